```python
import math
import jax, jax.numpy as jnp
from jax import lax
import numpy as np

D_MODEL = 1024
BATCH = 8
SEQ = 2048
DEPTH = 1

HEAD_DIM = 64
N_Q_HEADS = 8
N_KV_HEADS = 2
GROUP = N_Q_HEADS // N_KV_HEADS
WINDOW = 128
BLOCK = 128
ATTN_WIDTH = N_Q_HEADS * HEAD_DIM
KV_WIDTH = N_KV_HEADS * HEAD_DIM
CONV_CHANNELS = 512
CONV_GROUPS = 8
CONV_WIDTH = 31
N_BRANCHES = 2
Q_OFF = 0
K_OFF = Q_OFF + ATTN_WIDTH
V_OFF = K_OFF + KV_WIDTH
GLU_OFF = V_OFF + KV_WIDTH
GATE_OFF = GLU_OFF + 2 * CONV_CHANNELS
IN_WIDTH = GATE_OFF + N_BRANCHES * D_MODEL
D_FF = int(math.ceil(8 * D_MODEL / 3 / 256)) * 256
EPS = 1e-5
NEG = -1e30

kernel_name = "hybrid_swa_sink_conformer_conv_gated"


def rmsnorm(x, g):
    xf = x.astype(jnp.float32)
    y = xf * lax.rsqrt(jnp.mean(xf * xf, axis=-1, keepdims=True) + EPS)
    return (y * g.astype(jnp.float32)).astype(x.dtype)


def layernorm(x, g, b):
    xf = x.astype(jnp.float32)
    mu = jnp.mean(xf, axis=-1, keepdims=True)
    xc = xf - mu
    var = jnp.mean(xc * xc, axis=-1, keepdims=True)
    y = xc * lax.rsqrt(var + EPS) * g.astype(jnp.float32) + b.astype(jnp.float32)
    return y.astype(x.dtype)


def sliding_window_attention(q, k, v, sinks):
    B, S = q.shape[0], q.shape[1]
    nb = S // BLOCK
    qb = q.reshape(B, nb, BLOCK, N_KV_HEADS, GROUP, HEAD_DIM)

    def band(t):
        padded = jnp.pad(t, ((0, 0), (BLOCK, 0), (0, 0), (0, 0)))
        prev = padded[:, :S].reshape(B, nb, BLOCK, N_KV_HEADS, HEAD_DIM)
        cur = t.reshape(B, nb, BLOCK, N_KV_HEADS, HEAD_DIM)
        return jnp.concatenate([prev, cur], axis=2)

    kb = band(k)
    vb = band(v)
    scale = HEAD_DIM ** -0.5
    s = jnp.einsum('bnqhgd,bnkhd->bnhgqk', qb, kb).astype(jnp.float32) * scale
    qi = jnp.arange(BLOCK)[:, None]
    kj = jnp.arange(2 * BLOCK)[None, :]
    diff = qi + BLOCK - kj
    kpos = jnp.arange(nb)[:, None, None] * BLOCK - BLOCK + kj[None]
    valid = (diff >= 0)[None] & (diff < WINDOW)[None] & (kpos >= 0)
    s = jnp.where(valid[None, :, None, None], s, NEG)
    sink_col = jnp.broadcast_to(
        sinks.astype(jnp.float32).reshape(1, 1, N_KV_HEADS, GROUP, 1, 1),
        s.shape[:-1] + (1,))
    p = jax.nn.softmax(jnp.concatenate([s, sink_col], axis=-1), axis=-1)[..., :-1]
    o = jnp.einsum('bnhgqk,bnkhd->bnqhgd', p.astype(v.dtype), vb)
    return o.reshape(B, S, ATTN_WIDTH)


def conformer_conv(u, conv_w, conv_b, ln_g, ln_b):
    a, b = jnp.split(u, 2, axis=-1)
    z = a * jax.nn.sigmoid(b)
    z = lax.conv_general_dilated(
        z, conv_w[:, None, :].astype(z.dtype),
        window_strides=(1,), padding=[(CONV_WIDTH - 1, 0)],
        dimension_numbers=('NWC', 'WIO', 'NWC'),
        feature_group_count=CONV_CHANNELS) + conv_b
    z = layernorm(z, ln_g, ln_b)
    return jax.nn.silu(z)


def setup_inputs(seed: int = 0) -> dict:
    key = jax.random.key(seed)
    ks = jax.random.split(key, 20)
    L, D, C = DEPTH, D_MODEL, CONV_CHANNELS
    nrm = lambda k, shape, fan_in: jax.random.normal(k, shape, jnp.float32) * fan_in ** -0.5
    gain = lambda k, shape: 1.0 + 0.01 * jax.random.normal(k, shape, jnp.float32)
    small = lambda k, shape: 0.01 * jax.random.normal(k, shape, jnp.float32)
    return {
        "x": jax.random.normal(ks[0], (BATCH, SEQ, D), jnp.float32),
        "g_mix_norm": gain(ks[1], (L, D)),
        "w_in": nrm(ks[2], (L, D, IN_WIDTH), D),
        "b_in": small(ks[3], (L, IN_WIDTH)),
        "sinks": 0.5 * jax.random.normal(ks[4], (L, N_Q_HEADS), jnp.float32),
        "conv_w": nrm(ks[5], (L, CONV_WIDTH, C), CONV_WIDTH),
        "conv_b": small(ks[6], (L, C)),
        "ln_g": gain(ks[7], (L, C)),
        "ln_b": small(ks[8], (L, C)),
        "w_attn_proj": nrm(ks[9], (L, ATTN_WIDTH, D), ATTN_WIDTH),
        "w_conv_proj": nrm(ks[10], (L, C, D), C),
        "b_conv_proj": small(ks[11], (L, D)),
        "w_out": nrm(ks[12], (L, D, D), D),
        "g_ffn_norm": gain(ks[13], (L, D)),
        "w_ffn_in": nrm(ks[14], (L, D, 2 * D_FF), D),
        "w_ffn_down": nrm(ks[15], (L, D_FF, D), D_FF),
        "g_final": gain(ks[16], (D,)),
    }


def reference(x, g_mix_norm, w_in, b_in, sinks, conv_w, conv_b, ln_g, ln_b,
              w_attn_proj, w_conv_proj, b_conv_proj, w_out, g_ffn_norm,
              w_ffn_in, w_ffn_down, g_final):
    B, S, D = x.shape
    for l in range(DEPTH):
        h = rmsnorm(x, g_mix_norm[l])
        proj = h @ w_in[l] + b_in[l]
        q = proj[..., Q_OFF:K_OFF].reshape(B, S, N_Q_HEADS, HEAD_DIM)
        k = proj[..., K_OFF:V_OFF].reshape(B, S, N_KV_HEADS, HEAD_DIM)
        v = proj[..., V_OFF:GLU_OFF].reshape(B, S, N_KV_HEADS, HEAD_DIM)
        glu_in = proj[..., GLU_OFF:GATE_OFF]
        gates = jax.nn.sigmoid(proj[..., GATE_OFF:].reshape(B, S, N_BRANCHES, D))

        y_attn = sliding_window_attention(q, k, v, sinks[l]) @ w_attn_proj[l]
        y_conv = conformer_conv(glu_in, conv_w[l], conv_b[l], ln_g[l], ln_b[l]) @ w_conv_proj[l] + b_conv_proj[l]
        merged = gates[:, :, 0] * y_attn + gates[:, :, 1] * y_conv
        x = x + merged @ w_out[l]

        h2 = rmsnorm(x, g_ffn_norm[l])
        gu = h2 @ w_ffn_in[l]
        gate, up = gu[..., :D_FF], gu[..., D_FF:]
        x = x + (jax.nn.silu(gate) * up) @ w_ffn_down[l]
    return rmsnorm(x, g_final)
```

```python
import functools

import jax
import jax.numpy as jnp
from jax import lax
from jax.experimental import pallas as pl
from jax.experimental.pallas import tpu as pltpu

D_MODEL = 1024
HEAD_DIM = 64
N_Q_HEADS = 8
N_KV_HEADS = 2
GROUP = N_Q_HEADS // N_KV_HEADS
WINDOW = 128
ATTN_WIDTH = N_Q_HEADS * HEAD_DIM
KV_WIDTH = N_KV_HEADS * HEAD_DIM
CONV_CHANNELS = 512
CONV_WIDTH = 31
CONV_HALO = 32
K_OFF = ATTN_WIDTH
V_OFF = K_OFF + KV_WIDTH
GLU_OFF = V_OFF + KV_WIDTH
GATE_OFF = GLU_OFF + 2 * CONV_CHANNELS
IN_WIDTH = GATE_OFF + 2 * D_MODEL
D_FF = 2816
EPS = 1e-5
NEG = -1e30

SEQ_TILE = 256
FFN_CHUNK = 256
VMEM_LIMIT_BYTES = 56 * 1024 * 1024


def _rmsnorm(x, g):
    return x * lax.rsqrt(jnp.mean(x * x, axis=-1, keepdims=True) + EPS) * g


def _dot(a, b):
    return jnp.dot(a, b, preferred_element_type=jnp.float32)


def _layer_kernel(sinks_ref, x_ref, gmix_ref, win_ref, bin_ref, convw_ref, convb_ref,
                  lng_ref, lnb_ref, wap_ref, wcp_ref, bcp_ref, wout_ref, gffn_ref,
                  wfi_ref, wfd_ref, gfin_ref, o_ref, k_scr, v_scr, z_scr):
    T = SEQ_TILE
    s_idx = pl.program_id(1)

    @pl.when(s_idx == 0)
    def _():
        k_scr[0:WINDOW, :] = jnp.zeros((WINDOW, KV_WIDTH), k_scr.dtype)
        v_scr[0:WINDOW, :] = jnp.zeros((WINDOW, KV_WIDTH), v_scr.dtype)
        z_scr[0:CONV_HALO, :] = jnp.zeros((CONV_HALO, CONV_CHANNELS), z_scr.dtype)

    x = x_ref[0]
    h = _rmsnorm(x, gmix_ref[...]).astype(jnp.bfloat16)

    qkv = _dot(h, win_ref[:, 0:GLU_OFF]) + bin_ref[:, 0:GLU_OFF]
    q = (qkv[:, 0:K_OFF] * (HEAD_DIM ** -0.5)).astype(jnp.bfloat16)
    k_scr[WINDOW:WINDOW + T, :] = qkv[:, K_OFF:V_OFF].astype(jnp.bfloat16)
    v_scr[WINDOW:WINDOW + T, :] = qkv[:, V_OFF:GLU_OFF].astype(jnp.bfloat16)

    qi = lax.broadcasted_iota(jnp.int32, (WINDOW, 2 * WINDOW), 0)
    kj = lax.broadcasted_iota(jnp.int32, (WINDOW, 2 * WINDOW), 1)
    diff = qi + WINDOW - kj
    band = (diff >= 0) & (diff < WINDOW)
    attn_blocks = []
    for b in range(T // WINDOW):
        r0 = b * WINDOW
        if b == 0:
            valid = band & ((kj >= WINDOW) | (s_idx > 0))
        else:
            valid = band
        kb = k_scr[r0:r0 + 2 * WINDOW, :]
        vb = v_scr[r0:r0 + 2 * WINDOW, :]
        heads = []
        for hq in range(N_Q_HEADS):
            g = hq // GROUP
            qh = q[r0:r0 + WINDOW, hq * HEAD_DIM:(hq + 1) * HEAD_DIM]
            kh = kb[:, g * HEAD_DIM:(g + 1) * HEAD_DIM]
            vh = vb[:, g * HEAD_DIM:(g + 1) * HEAD_DIM]
            s = lax.dot_general(qh, kh, (((1,), (1,)), ((), ())),
                                preferred_element_type=jnp.float32)
            s = jnp.where(valid, s, NEG)
            sink = sinks_ref[hq]
            m = jnp.maximum(jnp.max(s, axis=-1, keepdims=True), sink)
            e = jnp.exp(s - m)
            denom = jnp.sum(e, axis=-1, keepdims=True) + jnp.exp(sink - m)
            o = _dot(e.astype(jnp.bfloat16), vh) / denom
            heads.append(o)
        attn_blocks.append(jnp.concatenate(heads, axis=-1))
    attn = jnp.concatenate(attn_blocks, axis=0).astype(jnp.bfloat16)
    y_attn = _dot(attn, wap_ref[...])

    k_scr[0:WINDOW, :] = k_scr[T:T + WINDOW, :]
    v_scr[0:WINDOW, :] = v_scr[T:T + WINDOW, :]

    glu = _dot(h, win_ref[:, GLU_OFF:GATE_OFF]) + bin_ref[:, GLU_OFF:GATE_OFF]
    z = glu[:, 0:CONV_CHANNELS] * jax.nn.sigmoid(glu[:, CONV_CHANNELS:])
    z_scr[CONV_HALO:CONV_HALO + T, :] = z
    acc = jnp.broadcast_to(convb_ref[...], (T, CONV_CHANNELS))
    for kk in range(CONV_WIDTH):
        off = CONV_HALO - (CONV_WIDTH - 1) + kk
        acc = acc + convw_ref[kk:kk + 1, :] * z_scr[off:off + T, :]
    z_scr[0:CONV_HALO, :] = z_scr[T:T + CONV_HALO, :]
    mu = jnp.mean(acc, axis=-1, keepdims=True)
    xc = acc - mu
    var = jnp.mean(xc * xc, axis=-1, keepdims=True)
    c = xc * lax.rsqrt(var + EPS) * lng_ref[...] + lnb_ref[...]
    c = (c * jax.nn.sigmoid(c)).astype(jnp.bfloat16)
    y_conv = _dot(c, wcp_ref[...]) + bcp_ref[...]

    gates = jax.nn.sigmoid(_dot(h, win_ref[:, GATE_OFF:IN_WIDTH]) + bin_ref[:, GATE_OFF:IN_WIDTH])
    merged = gates[:, 0:D_MODEL] * y_attn + gates[:, D_MODEL:] * y_conv
    x1 = x + _dot(merged.astype(jnp.bfloat16), wout_ref[...])

    h2 = _rmsnorm(x1, gffn_ref[...]).astype(jnp.bfloat16)
    ffn = jnp.zeros((T, D_MODEL), jnp.float32)
    for j in range(D_FF // FFN_CHUNK):
        c0 = j * FFN_CHUNK
        gate = _dot(h2, wfi_ref[:, c0:c0 + FFN_CHUNK])
        up = _dot(h2, wfi_ref[:, D_FF + c0:D_FF + c0 + FFN_CHUNK])
        act = (gate * jax.nn.sigmoid(gate) * up).astype(jnp.bfloat16)
        ffn = ffn + _dot(act, wfd_ref[c0:c0 + FFN_CHUNK, :])
    x2 = x1 + ffn
    o_ref[0] = _rmsnorm(x2, gfin_ref[...])


def _resident(shape):
    return pl.BlockSpec(shape, lambda b, s: (0,) * len(shape), pipeline_mode=pl.Buffered(1))


@jax.jit
def kernel(x, g_mix_norm, w_in, b_in, sinks, conv_w, conv_b, ln_g, ln_b, w_attn_proj,
           w_conv_proj, b_conv_proj, w_out, g_ffn_norm, w_ffn_in, w_ffn_down, g_final):
    B, S, D = x.shape
    assert D == D_MODEL and S % SEQ_TILE == 0 and SEQ_TILE % WINDOW == 0
    assert w_in.shape[0] == 1, "single-layer kernel"
    bf = jnp.bfloat16
    row = lambda a: a.reshape(1, -1)
    operands = (
        sinks[0],
        x,
        row(g_mix_norm[0]), w_in[0].astype(bf), row(b_in[0]),
        conv_w[0], row(conv_b[0]), row(ln_g[0]), row(ln_b[0]),
        w_attn_proj[0].astype(bf), w_conv_proj[0].astype(bf), row(b_conv_proj[0]),
        w_out[0].astype(bf), row(g_ffn_norm[0]),
        w_ffn_in[0].astype(bf), w_ffn_down[0].astype(bf), row(g_final),
    )
    x_spec = pl.BlockSpec((1, SEQ_TILE, D), lambda b, s: (b, s, 0))
    in_specs = [pl.BlockSpec(memory_space=pltpu.SMEM), x_spec]
    in_specs += [_resident(a.shape) for a in operands[2:]]
    return pl.pallas_call(
        _layer_kernel,
        out_shape=jax.ShapeDtypeStruct(x.shape, x.dtype),
        grid=(B, S // SEQ_TILE),
        in_specs=in_specs,
        out_specs=x_spec,
        scratch_shapes=[
            pltpu.VMEM((SEQ_TILE + WINDOW, KV_WIDTH), jnp.bfloat16),
            pltpu.VMEM((SEQ_TILE + WINDOW, KV_WIDTH), jnp.bfloat16),
            pltpu.VMEM((SEQ_TILE + CONV_HALO, CONV_CHANNELS), jnp.float32),
        ],
        compiler_params=pltpu.CompilerParams(
            dimension_semantics=("arbitrary", "arbitrary"),
            vmem_limit_bytes=VMEM_LIMIT_BYTES,
        ),
        name="hybrid_layer",
    )(*operands)
```

```python
import functools

import jax
import jax.numpy as jnp
from jax import lax
from jax.experimental import pallas as pl
from jax.experimental.pallas import tpu as pltpu

D_MODEL = 1024
HEAD_DIM = 64
N_Q_HEADS = 8
N_KV_HEADS = 2
GROUP = N_Q_HEADS // N_KV_HEADS
WINDOW = 128
ATTN_WIDTH = N_Q_HEADS * HEAD_DIM
KV_WIDTH = N_KV_HEADS * HEAD_DIM
CONV_CHANNELS = 512
CONV_WIDTH = 31
CONV_HALO = 32
K_OFF = ATTN_WIDTH
V_OFF = K_OFF + KV_WIDTH
GLU_OFF = V_OFF + KV_WIDTH
GATE_OFF = GLU_OFF + 2 * CONV_CHANNELS
IN_WIDTH = GATE_OFF + 2 * D_MODEL
D_FF = 2816
EPS = 1e-5
NEG = -1e30

LANES = 128
SUBLANES = 8
SEQ_TILE = 256
FFN_CHUNK = 256
N_FFN_CHUNKS = D_FF // FFN_CHUNK
CONV_ROWS = 128
VMEM_LIMIT_BYTES = 56 * 1024 * 1024


def _rmsnorm(x, g):
    return x * lax.rsqrt(jnp.mean(x * x, axis=-1, keepdims=True) + EPS) * g


def _dot(a, b):
    return jnp.dot(a, b, preferred_element_type=jnp.float32)


def _dot_nt(a, b):
    return lax.dot_general(a, b, (((1,), (1,)), ((), ())), preferred_element_type=jnp.float32)


def _attn_scores(q, kd_scr):
    T = q.shape[0]
    low_half = lax.broadcasted_iota(jnp.int32, (WINDOW, LANES), 1) < HEAD_DIM
    zero = jnp.zeros((), q.dtype)
    scores = {}
    for b in range(T // WINDOW):
        r0 = b * WINDOW
        for p in range(N_Q_HEADS // 2):
            g = (2 * p) // GROUP
            qp = q[r0:r0 + WINDOW, p * LANES:(p + 1) * LANES]
            rhs = jnp.concatenate([jnp.where(low_half, qp, zero),
                                   jnp.where(low_half, zero, qp)], axis=0)
            scores[b, p] = _dot_nt(kd_scr[g, r0:r0 + 2 * WINDOW, :], rhs)
    return scores


def _attn_probs(scores, sinks_ref, first_tile):
    kj = lax.broadcasted_iota(jnp.int32, (2 * WINDOW, 2 * LANES), 0)
    qi = lax.broadcasted_iota(jnp.int32, (2 * WINDOW, 2 * LANES), 1) % WINDOW
    diff = qi + WINDOW - kj
    band = (diff >= 0) & (diff < WINDOW)
    band_first = band & ((kj >= WINDOW) | jnp.logical_not(first_tile))
    head_lane = lax.broadcasted_iota(jnp.int32, (1, 2 * LANES), 1) < LANES
    probs = {}
    for (b, p), st in scores.items():
        st = jnp.where(band_first if b == 0 else band, st, NEG)
        sink = jnp.where(head_lane, sinks_ref[2 * p], sinks_ref[2 * p + 1])
        m = jnp.maximum(jnp.max(st, axis=0, keepdims=True), sink)
        e = jnp.exp(st - m)
        denom = jnp.sum(e, axis=0, keepdims=True) + jnp.exp(sink - m)
        probs[b, p] = (e.astype(jnp.bfloat16), denom)
    return probs


def _attn_values(probs, vt_scr, T):
    blocks = []
    for b in range(T // WINDOW):
        r0 = b * WINDOW
        pairs = []
        for p in range(N_Q_HEADS // 2):
            g = (2 * p) // GROUP
            e, denom = probs[b, p]
            vt = vt_scr[g * HEAD_DIM:(g + 1) * HEAD_DIM, r0:r0 + 2 * WINDOW]
            ot = _dot(vt, e) / denom
            pairs.append(jnp.concatenate([ot[:, 0:LANES], ot[:, LANES:]], axis=0).T)
        blocks.append(jnp.concatenate(pairs, axis=1))
    return jnp.concatenate(blocks, axis=0)


def _causal_conv(z_scr, convw_ref, convb_ref, T):
    first = CONV_HALO - (CONV_WIDTH - 1)
    col_chunks = []
    for c in range(CONV_CHANNELS // LANES):
        cs = slice(c * LANES, (c + 1) * LANES)
        row_chunks = []
        for r0 in range(0, T, CONV_ROWS):
            acc = jnp.broadcast_to(convb_ref[:, cs], (CONV_ROWS, LANES))
            for r in range(SUBLANES):
                rows = CONV_ROWS + (SUBLANES if r else 0)
                part = None
                for o in range(first, first + CONV_WIDTH):
                    if o % SUBLANES != r:
                        continue
                    term = convw_ref[o - first:o - first + 1, cs] * z_scr[r0 + o - r:r0 + o - r + rows, cs]
                    part = term if part is None else part + term
                acc = acc + part[r:r + CONV_ROWS, :]
            row_chunks.append(acc)
        col_chunks.append(jnp.concatenate(row_chunks, axis=0))
    return jnp.concatenate(col_chunks, axis=1)


def _layer_kernel(tiles_per_seq, sinks_ref, x_ref, gmix_ref, win_ref, bin_ref, convw_ref,
                  convb_ref, lng_ref, lnb_ref, wap_ref, wcp_ref, bcp_ref, wout_ref, gffn_ref,
                  wfi_ref, wfd_ref, gfin_ref, o_ref, kd_scr, vt_scr, z_scr, x1_scr):
    T = SEQ_TILE
    step = pl.program_id(0)
    first_tile = step % tiles_per_seq == 0
    slot = step % 2

    @pl.when(step == 0)
    def _():
        x1_scr[1] = jnp.zeros((T, D_MODEL), x1_scr.dtype)

    @pl.when(first_tile)
    def _():
        kd_scr[:, 0:WINDOW, :] = jnp.zeros((N_KV_HEADS, WINDOW, LANES), kd_scr.dtype)
        vt_scr[:, 0:WINDOW] = jnp.zeros((KV_WIDTH, WINDOW), vt_scr.dtype)
        z_scr[0:CONV_HALO, :] = jnp.zeros((CONV_HALO, CONV_CHANNELS), z_scr.dtype)

    x1_prev = x1_scr[1 - slot]
    h2 = _rmsnorm(x1_prev, gffn_ref[...]).astype(jnp.bfloat16)
    acts = {}
    ffn = [jnp.zeros((T, D_MODEL), jnp.float32)]

    def ffn_up(j):
        c0 = j * FFN_CHUNK
        gate = _dot(h2, wfi_ref[:, c0:c0 + FFN_CHUNK])
        up = _dot(h2, wfi_ref[:, D_FF + c0:D_FF + c0 + FFN_CHUNK])
        acts[j] = (gate * jax.nn.sigmoid(gate) * up).astype(jnp.bfloat16)

    def ffn_down(j):
        c0 = j * FFN_CHUNK
        ffn[0] = ffn[0] + _dot(acts.pop(j), wfd_ref[c0:c0 + FFN_CHUNK, :])

    def ffn_chunks(first, last):
        for j in range(first, last + 1):
            ffn_up(j)
            if j > 0:
                ffn_down(j - 1)

    x = x_ref[0]
    h = _rmsnorm(x, gmix_ref[...]).astype(jnp.bfloat16)

    glu = _dot(h, win_ref[:, GLU_OFF:GATE_OFF]) + bin_ref[:, GLU_OFF:GATE_OFF]
    qkv = _dot(h, win_ref[:, 0:GLU_OFF]) + bin_ref[:, 0:GLU_OFF]
    ffn_chunks(0, 0)

    z_scr[CONV_HALO:CONV_HALO + T, :] = glu[:, 0:CONV_CHANNELS] * jax.nn.sigmoid(glu[:, CONV_CHANNELS:])
    q = (qkv[:, 0:K_OFF] * (HEAD_DIM ** -0.5)).astype(jnp.bfloat16)
    k = qkv[:, K_OFF:V_OFF]
    k_swapped = pltpu.roll(k, HEAD_DIM, 1)
    low_half = lax.broadcasted_iota(jnp.int32, (T, LANES), 1) < HEAD_DIM
    kd_scr[0, WINDOW:WINDOW + T, :] = jnp.where(low_half, k, k_swapped).astype(jnp.bfloat16)
    kd_scr[1, WINDOW:WINDOW + T, :] = jnp.where(low_half, k_swapped, k).astype(jnp.bfloat16)
    vt_scr[:, WINDOW:WINDOW + T] = qkv[:, V_OFF:GLU_OFF].T.astype(jnp.bfloat16)

    scores = _attn_scores(q, kd_scr)
    ffn_chunks(1, 1)
    gates = jax.nn.sigmoid(_dot(h, win_ref[:, GATE_OFF:IN_WIDTH]) + bin_ref[:, GATE_OFF:IN_WIDTH])
    ffn_chunks(2, 2)
    probs = _attn_probs(scores, sinks_ref, first_tile)
    attn = _attn_values(probs, vt_scr, T)
    kd_scr[:, 0:WINDOW, :] = kd_scr[:, T:T + WINDOW, :]
    vt_scr[:, 0:WINDOW] = vt_scr[:, T:T + WINDOW]
    ffn_chunks(3, 3)
    y_attn = _dot(attn.astype(jnp.bfloat16), wap_ref[...])
    ffn_chunks(4, 5)

    acc = _causal_conv(z_scr, convw_ref, convb_ref, T)
    z_scr[0:CONV_HALO, :] = z_scr[T:T + CONV_HALO, :]
    mu = jnp.mean(acc, axis=-1, keepdims=True)
    xc = acc - mu
    var = jnp.mean(xc * xc, axis=-1, keepdims=True)
    c = xc * lax.rsqrt(var + EPS) * lng_ref[...] + lnb_ref[...]
    c = (c * jax.nn.sigmoid(c)).astype(jnp.bfloat16)
    y_conv = _dot(c, wcp_ref[...]) + bcp_ref[...]
    ffn_chunks(6, 6)

    merged = gates[:, 0:D_MODEL] * y_attn + gates[:, D_MODEL:] * y_conv
    x1_scr[slot] = x + _dot(merged.astype(jnp.bfloat16), wout_ref[...])
    ffn_chunks(7, N_FFN_CHUNKS - 1)
    ffn_down(N_FFN_CHUNKS - 1)

    o_ref[0] = _rmsnorm(x1_prev + ffn[0], gfin_ref[...])


def _resident(shape):
    return pl.BlockSpec(shape, lambda i: (0,) * len(shape), pipeline_mode=pl.Buffered(1))


@jax.jit
def kernel(x, g_mix_norm, w_in, b_in, sinks, conv_w, conv_b, ln_g, ln_b, w_attn_proj,
           w_conv_proj, b_conv_proj, w_out, g_ffn_norm, w_ffn_in, w_ffn_down, g_final):
    B, S, D = x.shape
    assert D == D_MODEL and S % SEQ_TILE == 0 and SEQ_TILE % WINDOW == 0
    assert w_in.shape[0] == 1, "single-layer kernel"
    tiles_per_seq = S // SEQ_TILE
    n_tiles = B * tiles_per_seq
    bf = jnp.bfloat16
    row = lambda a: a.reshape(1, -1)
    operands = (
        sinks[0],
        x,
        row(g_mix_norm[0]), w_in[0].astype(bf), row(b_in[0]),
        conv_w[0], row(conv_b[0]), row(ln_g[0]), row(ln_b[0]),
        w_attn_proj[0].astype(bf), w_conv_proj[0].astype(bf), row(b_conv_proj[0]),
        w_out[0].astype(bf), row(g_ffn_norm[0]),
        w_ffn_in[0].astype(bf), w_ffn_down[0].astype(bf), row(g_final),
    )

    def tile_block(t):
        return (t // tiles_per_seq, t % tiles_per_seq, 0)

    x_spec = pl.BlockSpec((1, SEQ_TILE, D), lambda i: tile_block(jnp.minimum(i, n_tiles - 1)))
    o_spec = pl.BlockSpec((1, SEQ_TILE, D), lambda i: tile_block(jnp.maximum(i - 1, 0)))
    in_specs = [pl.BlockSpec(memory_space=pltpu.SMEM), x_spec]
    in_specs += [_resident(a.shape) for a in operands[2:]]
    return pl.pallas_call(
        functools.partial(_layer_kernel, tiles_per_seq),
        out_shape=jax.ShapeDtypeStruct(x.shape, x.dtype),
        grid=(n_tiles + 1,),
        in_specs=in_specs,
        out_specs=o_spec,
        scratch_shapes=[
            pltpu.VMEM((N_KV_HEADS, WINDOW + SEQ_TILE, LANES), jnp.bfloat16),
            pltpu.VMEM((KV_WIDTH, WINDOW + SEQ_TILE), jnp.bfloat16),
            pltpu.VMEM((SEQ_TILE + CONV_HALO, CONV_CHANNELS), jnp.float32),
            pltpu.VMEM((2, SEQ_TILE, D_MODEL), jnp.float32),
        ],
        compiler_params=pltpu.CompilerParams(
            dimension_semantics=("arbitrary",),
            vmem_limit_bytes=VMEM_LIMIT_BYTES,
        ),
        name="hybrid_layer",
    )(*operands)
```

```python
import functools

import jax
import jax.numpy as jnp
from jax import lax
from jax.experimental import pallas as pl
from jax.experimental.pallas import tpu as pltpu

D_MODEL = 1024
HEAD_DIM = 64
N_Q_HEADS = 8
N_KV_HEADS = 2
GROUP = N_Q_HEADS // N_KV_HEADS
WINDOW = 128
ATTN_WIDTH = N_Q_HEADS * HEAD_DIM
KV_WIDTH = N_KV_HEADS * HEAD_DIM
CONV_CHANNELS = 512
CONV_WIDTH = 31
CONV_HALO = 32
K_OFF = ATTN_WIDTH
V_OFF = K_OFF + KV_WIDTH
GLU_OFF = V_OFF + KV_WIDTH
GATE_OFF = GLU_OFF + 2 * CONV_CHANNELS
IN_WIDTH = GATE_OFF + 2 * D_MODEL
D_FF = 2816
EPS = 1e-5
NEG = -1e30

LANES = 128
SUBLANES = 8
SEQ_TILE = 256
FFN_CHUNK = 256
N_FFN_CHUNKS = D_FF // FFN_CHUNK
CONV_ROWS = 128
VMEM_LIMIT_BYTES = 56 * 1024 * 1024


def _rmsnorm(x, g):
    return x * lax.rsqrt(jnp.mean(x * x, axis=-1, keepdims=True) + EPS) * g


def _dot(a, b):
    return jnp.dot(a, b, preferred_element_type=jnp.float32)


def _dot_nt(a, b):
    return lax.dot_general(a, b, (((1,), (1,)), ((), ())), preferred_element_type=jnp.float32)


def _attn_scores(q, kd_scr):
    T = q.shape[0]
    low_half = lax.broadcasted_iota(jnp.int32, (WINDOW, LANES), 1) < HEAD_DIM
    zero = jnp.zeros((), q.dtype)
    scores = {}
    for b in range(T // WINDOW):
        r0 = b * WINDOW
        for p in range(N_Q_HEADS // 2):
            g = (2 * p) // GROUP
            qp = q[r0:r0 + WINDOW, p * LANES:(p + 1) * LANES]
            rhs = jnp.concatenate([jnp.where(low_half, qp, zero),
                                   jnp.where(low_half, zero, qp)], axis=0)
            scores[b, p] = _dot_nt(kd_scr[g, r0:r0 + 2 * WINDOW, :], rhs)
    return scores


def _attn_probs(scores, sinks_ref, first_tile):
    kj = lax.broadcasted_iota(jnp.int32, (2 * WINDOW, 2 * LANES), 0)
    qi = lax.broadcasted_iota(jnp.int32, (2 * WINDOW, 2 * LANES), 1) % WINDOW
    diff = qi + WINDOW - kj
    band = (diff >= 0) & (diff < WINDOW)
    band_first = band & ((kj >= WINDOW) | jnp.logical_not(first_tile))
    head_lane = lax.broadcasted_iota(jnp.int32, (1, 2 * LANES), 1) < LANES
    probs = {}
    for (b, p), st in scores.items():
        st = jnp.where(band_first if b == 0 else band, st, NEG)
        sink = jnp.where(head_lane, sinks_ref[2 * p], sinks_ref[2 * p + 1])
        m = jnp.maximum(jnp.max(st, axis=0, keepdims=True), sink)
        e = jnp.exp(st - m)
        denom = jnp.sum(e, axis=0, keepdims=True) + jnp.exp(sink - m)
        probs[b, p] = (e.astype(jnp.bfloat16), denom)
    return probs


def _attn_values(probs, vt_scr, T):
    blocks = []
    for b in range(T // WINDOW):
        r0 = b * WINDOW
        pairs = []
        for p in range(N_Q_HEADS // 2):
            g = (2 * p) // GROUP
            e, denom = probs[b, p]
            vt = vt_scr[g * HEAD_DIM:(g + 1) * HEAD_DIM, r0:r0 + 2 * WINDOW]
            ot = _dot(vt, e) / denom
            pairs.append(jnp.concatenate([ot[:, 0:LANES], ot[:, LANES:]], axis=0).T)
        blocks.append(jnp.concatenate(pairs, axis=1))
    return jnp.concatenate(blocks, axis=0)


def _conv_chunk(z_scr, convw_ref, convb_ref, r0, c):
    first = CONV_HALO - (CONV_WIDTH - 1)
    cs = slice(c * LANES, (c + 1) * LANES)
    acc = jnp.broadcast_to(convb_ref[:, cs], (CONV_ROWS, LANES))
    for r in range(SUBLANES):
        rows = CONV_ROWS + (SUBLANES if r else 0)
        part = None
        for o in range(first, first + CONV_WIDTH):
            if o % SUBLANES != r:
                continue
            term = convw_ref[o - first:o - first + 1, cs] * z_scr[r0 + o - r:r0 + o - r + rows, cs]
            part = term if part is None else part + term
        acc = acc + part[r:r + CONV_ROWS, :]
    return acc


def _layer_kernel(tiles_per_seq, sinks_ref, x_ref, gmix_ref, win_ref, bin_ref, convw_ref,
                  convb_ref, lng_ref, lnb_ref, wap_ref, wcp_ref, bcp_ref, wout_ref, gffn_ref,
                  wfi_ref, wfd_ref, gfin_ref, o_ref, kd_scr, vt_scr, z_scr, x1_scr, h2_scr,
                  act_scr, x2_scr):
    T = SEQ_TILE
    step = pl.program_id(0)
    first_tile = step % tiles_per_seq == 0

    @pl.when(step == 0)
    def _():
        x1_scr[...] = jnp.zeros(x1_scr.shape, x1_scr.dtype)
        h2_scr[...] = jnp.zeros(h2_scr.shape, h2_scr.dtype)
        x2_scr[...] = jnp.zeros(x2_scr.shape, x2_scr.dtype)

    @pl.when(first_tile)
    def _():
        kd_scr[:, 0:WINDOW, :] = jnp.zeros((N_KV_HEADS, WINDOW, LANES), kd_scr.dtype)
        vt_scr[:, 0:WINDOW] = jnp.zeros((KV_WIDTH, WINDOW), vt_scr.dtype)
        z_scr[0:CONV_HALO, :] = jnp.zeros((CONV_HALO, CONV_CHANNELS), z_scr.dtype)


    def ffn_up(j):
        c0 = j * FFN_CHUNK
        h2 = h2_scr[...]
        gate = _dot(h2, wfi_ref[:, c0:c0 + FFN_CHUNK])
        up = _dot(h2, wfi_ref[:, D_FF + c0:D_FF + c0 + FFN_CHUNK])
        act_scr[:, c0:c0 + FFN_CHUNK] = (gate * jax.nn.sigmoid(gate) * up).astype(jnp.bfloat16)

    assert N_FFN_CHUNKS == 11, "the matmul sequence below places chunks 0..10 by hand"
    o_ref[0] = _rmsnorm(x2_scr[...], gfin_ref[...])
    ffn_up(0)
    ffn_up(1)
    x = x_ref[0]
    h = _rmsnorm(x, gmix_ref[...]).astype(jnp.bfloat16)
    glu = _dot(h, win_ref[:, GLU_OFF:GATE_OFF]) + bin_ref[:, GLU_OFF:GATE_OFF]
    qkv = _dot(h, win_ref[:, 0:GLU_OFF]) + bin_ref[:, 0:GLU_OFF]
    z_scr[CONV_HALO:CONV_HALO + T, :] = glu[:, 0:CONV_CHANNELS] * jax.nn.sigmoid(glu[:, CONV_CHANNELS:])
    gates_pre = _dot(h, win_ref[:, GATE_OFF:IN_WIDTH]) + bin_ref[:, GATE_OFF:IN_WIDTH]
    conv = {(r0, c): _conv_chunk(z_scr, convw_ref, convb_ref, r0, c)
            for r0 in range(0, T, CONV_ROWS) for c in range(CONV_CHANNELS // LANES)}
    z_scr[0:CONV_HALO, :] = z_scr[T:T + CONV_HALO, :]
    ffn_up(2)

    q = (qkv[:, 0:K_OFF] * (HEAD_DIM ** -0.5)).astype(jnp.bfloat16)
    k = qkv[:, K_OFF:V_OFF]
    k_swapped = pltpu.roll(k, HEAD_DIM, 1)
    low_half = lax.broadcasted_iota(jnp.int32, (T, LANES), 1) < HEAD_DIM
    kd_scr[0, WINDOW:WINDOW + T, :] = jnp.where(low_half, k, k_swapped).astype(jnp.bfloat16)
    kd_scr[1, WINDOW:WINDOW + T, :] = jnp.where(low_half, k_swapped, k).astype(jnp.bfloat16)
    vt_scr[:, WINDOW:WINDOW + T] = qkv[:, V_OFF:GLU_OFF].T.astype(jnp.bfloat16)
    scores = _attn_scores(q, kd_scr)
    ffn_up(3)
    ffn_up(4)
    probs = _attn_probs(scores, sinks_ref, first_tile)
    attn = _attn_values(probs, vt_scr, T)
    kd_scr[:, 0:WINDOW, :] = kd_scr[:, T:T + WINDOW, :]
    vt_scr[:, 0:WINDOW] = vt_scr[:, T:T + WINDOW]
    ffn_up(5)
    gates = jax.nn.sigmoid(gates_pre)
    y_attn = _dot(attn.astype(jnp.bfloat16), wap_ref[...])

    acc = jnp.concatenate(
        [jnp.concatenate([conv[r0, c] for c in range(CONV_CHANNELS // LANES)], axis=1)
         for r0 in range(0, T, CONV_ROWS)], axis=0)
    mu = jnp.mean(acc, axis=-1, keepdims=True)
    xc = acc - mu
    var = jnp.mean(xc * xc, axis=-1, keepdims=True)
    cn = xc * lax.rsqrt(var + EPS) * lng_ref[...] + lnb_ref[...]
    cn = (cn * jax.nn.sigmoid(cn)).astype(jnp.bfloat16)
    ffn_up(6)
    y_conv = _dot(cn, wcp_ref[...]) + bcp_ref[...]
    ffn_up(7)
    ffn_up(8)
    merged = gates[:, 0:D_MODEL] * y_attn + gates[:, D_MODEL:] * y_conv
    ffn_up(9)
    ffn_up(10)
    x1 = x + _dot(merged.astype(jnp.bfloat16), wout_ref[...])
    h2_scr[...] = _rmsnorm(x1, gffn_ref[...]).astype(jnp.bfloat16)

    x2_scr[...] = x1_scr[...] + _dot(act_scr[...], wfd_ref[...])
    x1_scr[...] = x1


def _resident(shape):
    return pl.BlockSpec(shape, lambda i: (0,) * len(shape), pipeline_mode=pl.Buffered(1))


@jax.jit
def kernel(x, g_mix_norm, w_in, b_in, sinks, conv_w, conv_b, ln_g, ln_b, w_attn_proj,
           w_conv_proj, b_conv_proj, w_out, g_ffn_norm, w_ffn_in, w_ffn_down, g_final):
    B, S, D = x.shape
    assert D == D_MODEL and S % SEQ_TILE == 0 and SEQ_TILE % WINDOW == 0
    assert w_in.shape[0] == 1, "single-layer kernel"
    tiles_per_seq = S // SEQ_TILE
    n_tiles = B * tiles_per_seq
    bf = jnp.bfloat16
    row = lambda a: a.reshape(1, -1)
    operands = (
        sinks[0],
        x,
        row(g_mix_norm[0]), w_in[0].astype(bf), row(b_in[0]),
        conv_w[0], row(conv_b[0]), row(ln_g[0]), row(ln_b[0]),
        w_attn_proj[0].astype(bf), w_conv_proj[0].astype(bf), row(b_conv_proj[0]),
        w_out[0].astype(bf), row(g_ffn_norm[0]),
        w_ffn_in[0].astype(bf), w_ffn_down[0].astype(bf), row(g_final),
    )

    def tile_block(t):
        return (t // tiles_per_seq, t % tiles_per_seq, 0)

    x_spec = pl.BlockSpec((1, SEQ_TILE, D), lambda i: tile_block(jnp.minimum(i, n_tiles - 1)))
    o_spec = pl.BlockSpec((1, SEQ_TILE, D), lambda i: tile_block(jnp.maximum(i - 2, 0)))
    in_specs = [pl.BlockSpec(memory_space=pltpu.SMEM), x_spec]
    in_specs += [_resident(a.shape) for a in operands[2:]]
    return pl.pallas_call(
        functools.partial(_layer_kernel, tiles_per_seq),
        out_shape=jax.ShapeDtypeStruct(x.shape, x.dtype),
        grid=(n_tiles + 2,),
        in_specs=in_specs,
        out_specs=o_spec,
        scratch_shapes=[
            pltpu.VMEM((N_KV_HEADS, WINDOW + SEQ_TILE, LANES), jnp.bfloat16),
            pltpu.VMEM((KV_WIDTH, WINDOW + SEQ_TILE), jnp.bfloat16),
            pltpu.VMEM((SEQ_TILE + CONV_HALO, CONV_CHANNELS), jnp.float32),
            pltpu.VMEM((SEQ_TILE, D_MODEL), jnp.float32),
            pltpu.VMEM((SEQ_TILE, D_MODEL), jnp.bfloat16),
            pltpu.VMEM((SEQ_TILE, D_FF), jnp.bfloat16),
            pltpu.VMEM((SEQ_TILE, D_MODEL), jnp.float32),
        ],
        compiler_params=pltpu.CompilerParams(
            dimension_semantics=("arbitrary",),
            vmem_limit_bytes=VMEM_LIMIT_BYTES,
        ),
        name="hybrid_layer",
    )(*operands)
```

```python
import functools

import jax
import jax.numpy as jnp
from jax import lax
from jax.experimental import pallas as pl
from jax.experimental.pallas import tpu as pltpu

D_MODEL = 1024
HEAD_DIM = 64
N_Q_HEADS = 8
N_KV_HEADS = 2
GROUP = N_Q_HEADS // N_KV_HEADS
WINDOW = 128
ATTN_WIDTH = N_Q_HEADS * HEAD_DIM
KV_WIDTH = N_KV_HEADS * HEAD_DIM
CONV_CHANNELS = 512
CONV_WIDTH = 31
CONV_HALO = 32
K_OFF = ATTN_WIDTH
V_OFF = K_OFF + KV_WIDTH
GLU_OFF = V_OFF + KV_WIDTH
GATE_OFF = GLU_OFF + 2 * CONV_CHANNELS
IN_WIDTH = GATE_OFF + 2 * D_MODEL
D_FF = 2816
EPS = 1e-5
NEG = -1e30

LANES = 128
SUBLANES = 8
SEQ_TILE = 256
FFN_CHUNK = 256
N_FFN_CHUNKS = D_FF // FFN_CHUNK
CONV_ROWS = 128
STAGE_ROWS, STAGE_COLS = 512, 1024
STAGE_SLOTS = 3
VMEM_LIMIT_BYTES = 56 * 1024 * 1024


def _rmsnorm(x, g):
    return x * lax.rsqrt(jnp.mean(x * x, axis=-1, keepdims=True) + EPS) * g


def _dot(a, b):
    return jnp.dot(a, b, preferred_element_type=jnp.float32)


def _dot_nt(a, b):
    return lax.dot_general(a, b, (((1,), (1,)), ((), ())), preferred_element_type=jnp.float32)


def _load_weights(pairs, stage, sem):
    blocks = []
    for src, dst in pairs:
        rows, cols = src.shape
        assert dst.shape == src.shape
        for r0 in range(0, rows, STAGE_ROWS):
            for c0 in range(0, cols, STAGE_COLS):
                blocks.append((src, dst, r0, min(STAGE_ROWS, rows - r0), c0, min(STAGE_COLS, cols - c0)))

    def copy(i):
        src, _, r0, nr, c0, nc = blocks[i]
        slot = i % STAGE_SLOTS
        return pltpu.make_async_copy(src.at[r0:r0 + nr, c0:c0 + nc],
                                     stage.at[slot, 0:nr, 0:nc], sem.at[slot])

    for i in range(min(STAGE_SLOTS - 1, len(blocks))):
        copy(i).start()
    for i, (_, dst, r0, nr, c0, nc) in enumerate(blocks):
        if i + STAGE_SLOTS - 1 < len(blocks):
            copy(i + STAGE_SLOTS - 1).start()
        copy(i).wait()
        dst[r0:r0 + nr, c0:c0 + nc] = stage[i % STAGE_SLOTS, 0:nr, 0:nc].astype(dst.dtype)


def _attn_scores(q, kd_scr):
    T = q.shape[0]
    low_half = lax.broadcasted_iota(jnp.int32, (WINDOW, LANES), 1) < HEAD_DIM
    zero = jnp.zeros((), q.dtype)
    scores = {}
    for b in range(T // WINDOW):
        r0 = b * WINDOW
        for p in range(N_Q_HEADS // 2):
            g = (2 * p) // GROUP
            qp = q[r0:r0 + WINDOW, p * LANES:(p + 1) * LANES]
            rhs = jnp.concatenate([jnp.where(low_half, qp, zero),
                                   jnp.where(low_half, zero, qp)], axis=0)
            scores[b, p] = _dot_nt(kd_scr[g, r0:r0 + 2 * WINDOW, :], rhs)
    return scores


def _attn_probs(scores, sinks_ref, first_tile):
    kj = lax.broadcasted_iota(jnp.int32, (2 * WINDOW, 2 * LANES), 0)
    qi = lax.broadcasted_iota(jnp.int32, (2 * WINDOW, 2 * LANES), 1) % WINDOW
    diff = qi + WINDOW - kj
    band = (diff >= 0) & (diff < WINDOW)
    band_first = band & ((kj >= WINDOW) | jnp.logical_not(first_tile))
    head_lane = lax.broadcasted_iota(jnp.int32, (1, 2 * LANES), 1) < LANES
    probs = {}
    for (b, p), st in scores.items():
        st = jnp.where(band_first if b == 0 else band, st, NEG)
        sink = jnp.where(head_lane, sinks_ref[2 * p], sinks_ref[2 * p + 1])
        m = jnp.maximum(jnp.max(st, axis=0, keepdims=True), sink)
        e = jnp.exp(st - m)
        denom = jnp.sum(e, axis=0, keepdims=True) + jnp.exp(sink - m)
        probs[b, p] = (e.astype(jnp.bfloat16), denom)
    return probs


def _attn_values(probs, vt_scr, T):
    blocks = []
    for b in range(T // WINDOW):
        r0 = b * WINDOW
        pairs = []
        for p in range(N_Q_HEADS // 2):
            g = (2 * p) // GROUP
            e, denom = probs[b, p]
            vt = vt_scr[g * HEAD_DIM:(g + 1) * HEAD_DIM, r0:r0 + 2 * WINDOW]
            ot = _dot(vt, e) / denom
            pairs.append(jnp.concatenate([ot[:, 0:LANES], ot[:, LANES:]], axis=0).T)
        blocks.append(jnp.concatenate(pairs, axis=1))
    return jnp.concatenate(blocks, axis=0)


def _causal_conv(z_scr, convw_ref, convb_ref, T):
    first = CONV_HALO - (CONV_WIDTH - 1)
    col_chunks = []
    for c in range(CONV_CHANNELS // LANES):
        cs = slice(c * LANES, (c + 1) * LANES)
        row_chunks = []
        for r0 in range(0, T, CONV_ROWS):
            acc = jnp.broadcast_to(convb_ref[:, cs], (CONV_ROWS, LANES))
            for r in range(SUBLANES):
                rows = CONV_ROWS + (SUBLANES if r else 0)
                part = None
                for o in range(first, first + CONV_WIDTH):
                    if o % SUBLANES != r:
                        continue
                    term = convw_ref[o - first:o - first + 1, cs] * z_scr[r0 + o - r:r0 + o - r + rows, cs]
                    part = term if part is None else part + term
                acc = acc + part[r:r + CONV_ROWS, :]
            row_chunks.append(acc)
        col_chunks.append(jnp.concatenate(row_chunks, axis=0))
    return jnp.concatenate(col_chunks, axis=1)


def _layer_kernel(tiles_per_seq, sinks_ref, x_ref, gmix_ref, win_hbm, bin_ref, convw_ref,
                  convb_ref, lng_ref, lnb_ref, wap_hbm, wcp_hbm, bcp_ref, wout_hbm, gffn_ref,
                  wfi_hbm, wfd_hbm, gfin_ref, o_ref, kd_scr, vt_scr, z_scr, x1_scr,
                  win_ref, wap_ref, wcp_ref, wout_ref, wfi_ref, wfd_ref, stage, stage_sem):
    T = SEQ_TILE
    step = pl.program_id(0)
    first_tile = step % tiles_per_seq == 0
    slot = step % 2

    @pl.when(step == 0)
    def _():
        _load_weights([(win_hbm, win_ref), (wap_hbm, wap_ref), (wcp_hbm, wcp_ref),
                       (wout_hbm, wout_ref), (wfi_hbm, wfi_ref), (wfd_hbm, wfd_ref)],
                      stage, stage_sem)
        x1_scr[1] = jnp.zeros((T, D_MODEL), x1_scr.dtype)

    @pl.when(first_tile)
    def _():
        kd_scr[:, 0:WINDOW, :] = jnp.zeros((N_KV_HEADS, WINDOW, LANES), kd_scr.dtype)
        vt_scr[:, 0:WINDOW] = jnp.zeros((KV_WIDTH, WINDOW), vt_scr.dtype)
        z_scr[0:CONV_HALO, :] = jnp.zeros((CONV_HALO, CONV_CHANNELS), z_scr.dtype)

    x1_prev = x1_scr[1 - slot]
    h2 = _rmsnorm(x1_prev, gffn_ref[...]).astype(jnp.bfloat16)
    acts = {}
    ffn = [jnp.zeros((T, D_MODEL), jnp.float32)]

    def ffn_up(j):
        c0 = j * FFN_CHUNK
        gate = _dot(h2, wfi_ref[:, c0:c0 + FFN_CHUNK])
        up = _dot(h2, wfi_ref[:, D_FF + c0:D_FF + c0 + FFN_CHUNK])
        acts[j] = (gate * jax.nn.sigmoid(gate) * up).astype(jnp.bfloat16)

    def ffn_down(j):
        c0 = j * FFN_CHUNK
        ffn[0] = ffn[0] + _dot(acts.pop(j), wfd_ref[c0:c0 + FFN_CHUNK, :])

    def ffn_chunks(first, last):
        for j in range(first, last + 1):
            ffn_up(j)
            if j > 0:
                ffn_down(j - 1)

    x = x_ref[0]
    h = _rmsnorm(x, gmix_ref[...]).astype(jnp.bfloat16)

    glu = _dot(h, win_ref[:, GLU_OFF:GATE_OFF]) + bin_ref[:, GLU_OFF:GATE_OFF]
    qkv = _dot(h, win_ref[:, 0:GLU_OFF]) + bin_ref[:, 0:GLU_OFF]
    ffn_chunks(0, 0)

    z_scr[CONV_HALO:CONV_HALO + T, :] = glu[:, 0:CONV_CHANNELS] * jax.nn.sigmoid(glu[:, CONV_CHANNELS:])
    q = (qkv[:, 0:K_OFF] * (HEAD_DIM ** -0.5)).astype(jnp.bfloat16)
    k = qkv[:, K_OFF:V_OFF]
    k_swapped = pltpu.roll(k, HEAD_DIM, 1)
    low_half = lax.broadcasted_iota(jnp.int32, (T, LANES), 1) < HEAD_DIM
    kd_scr[0, WINDOW:WINDOW + T, :] = jnp.where(low_half, k, k_swapped).astype(jnp.bfloat16)
    kd_scr[1, WINDOW:WINDOW + T, :] = jnp.where(low_half, k_swapped, k).astype(jnp.bfloat16)
    vt_scr[:, WINDOW:WINDOW + T] = qkv[:, V_OFF:GLU_OFF].T.astype(jnp.bfloat16)

    scores = _attn_scores(q, kd_scr)
    ffn_chunks(1, 1)
    gates = jax.nn.sigmoid(_dot(h, win_ref[:, GATE_OFF:IN_WIDTH]) + bin_ref[:, GATE_OFF:IN_WIDTH])
    ffn_chunks(2, 2)
    probs = _attn_probs(scores, sinks_ref, first_tile)
    attn = _attn_values(probs, vt_scr, T)
    kd_scr[:, 0:WINDOW, :] = kd_scr[:, T:T + WINDOW, :]
    vt_scr[:, 0:WINDOW] = vt_scr[:, T:T + WINDOW]
    ffn_chunks(3, 3)
    y_attn = _dot(attn.astype(jnp.bfloat16), wap_ref[...])
    ffn_chunks(4, 5)

    acc = _causal_conv(z_scr, convw_ref, convb_ref, T)
    z_scr[0:CONV_HALO, :] = z_scr[T:T + CONV_HALO, :]
    mu = jnp.mean(acc, axis=-1, keepdims=True)
    xc = acc - mu
    var = jnp.mean(xc * xc, axis=-1, keepdims=True)
    c = xc * lax.rsqrt(var + EPS) * lng_ref[...] + lnb_ref[...]
    c = (c * jax.nn.sigmoid(c)).astype(jnp.bfloat16)
    y_conv = _dot(c, wcp_ref[...]) + bcp_ref[...]
    ffn_chunks(6, 6)

    merged = gates[:, 0:D_MODEL] * y_attn + gates[:, D_MODEL:] * y_conv
    x1_scr[slot] = x + _dot(merged.astype(jnp.bfloat16), wout_ref[...])
    ffn_chunks(7, N_FFN_CHUNKS - 1)
    ffn_down(N_FFN_CHUNKS - 1)

    o_ref[0] = _rmsnorm(x1_prev + ffn[0], gfin_ref[...])


def _resident(shape):
    return pl.BlockSpec(shape, lambda i: (0,) * len(shape), pipeline_mode=pl.Buffered(1))


@jax.jit
def kernel(x, g_mix_norm, w_in, b_in, sinks, conv_w, conv_b, ln_g, ln_b, w_attn_proj,
           w_conv_proj, b_conv_proj, w_out, g_ffn_norm, w_ffn_in, w_ffn_down, g_final):
    B, S, D = x.shape
    assert D == D_MODEL and S % SEQ_TILE == 0 and SEQ_TILE % WINDOW == 0
    assert w_in.shape[0] == 1, "single-layer kernel"
    tiles_per_seq = S // SEQ_TILE
    n_tiles = B * tiles_per_seq
    row = lambda a: a.reshape(1, -1)
    weights = (w_in[0], w_attn_proj[0], w_conv_proj[0], w_out[0], w_ffn_in[0], w_ffn_down[0])
    operands = (
        sinks[0],
        x,
        row(g_mix_norm[0]), weights[0], row(b_in[0]),
        conv_w[0], row(conv_b[0]), row(ln_g[0]), row(ln_b[0]),
        weights[1], weights[2], row(b_conv_proj[0]),
        weights[3], row(g_ffn_norm[0]),
        weights[4], weights[5], row(g_final),
    )

    def tile_block(t):
        return (t // tiles_per_seq, t % tiles_per_seq, 0)

    x_spec = pl.BlockSpec((1, SEQ_TILE, D), lambda i: tile_block(jnp.minimum(i, n_tiles - 1)))
    o_spec = pl.BlockSpec((1, SEQ_TILE, D), lambda i: tile_block(jnp.maximum(i - 1, 0)))
    in_specs = [pl.BlockSpec(memory_space=pltpu.SMEM), x_spec]
    in_specs += [pl.BlockSpec(memory_space=pl.ANY) if any(a is w for w in weights) else _resident(a.shape)
                 for a in operands[2:]]
    return pl.pallas_call(
        functools.partial(_layer_kernel, tiles_per_seq),
        out_shape=jax.ShapeDtypeStruct(x.shape, x.dtype),
        grid=(n_tiles + 1,),
        in_specs=in_specs,
        out_specs=o_spec,
        scratch_shapes=[
            pltpu.VMEM((N_KV_HEADS, WINDOW + SEQ_TILE, LANES), jnp.bfloat16),
            pltpu.VMEM((KV_WIDTH, WINDOW + SEQ_TILE), jnp.bfloat16),
            pltpu.VMEM((SEQ_TILE + CONV_HALO, CONV_CHANNELS), jnp.float32),
            pltpu.VMEM((2, SEQ_TILE, D_MODEL), jnp.float32),
            *[pltpu.VMEM(w.shape, jnp.bfloat16) for w in weights],
            pltpu.VMEM((STAGE_SLOTS, STAGE_ROWS, STAGE_COLS), jnp.float32),
            pltpu.SemaphoreType.DMA((STAGE_SLOTS,)),
        ],
        compiler_params=pltpu.CompilerParams(
            dimension_semantics=("arbitrary",),
            vmem_limit_bytes=VMEM_LIMIT_BYTES,
        ),
        name="hybrid_layer",
    )(*operands)
```

```python
import functools

import jax
import jax.numpy as jnp
from jax import lax
from jax.experimental import pallas as pl
from jax.experimental.pallas import tpu as pltpu

D_MODEL = 1024
HEAD_DIM = 64
N_Q_HEADS = 8
N_KV_HEADS = 2
GROUP = N_Q_HEADS // N_KV_HEADS
WINDOW = 128
ATTN_WIDTH = N_Q_HEADS * HEAD_DIM
KV_WIDTH = N_KV_HEADS * HEAD_DIM
CONV_CHANNELS = 512
CONV_WIDTH = 31
CONV_HALO = 32
K_OFF = ATTN_WIDTH
V_OFF = K_OFF + KV_WIDTH
GLU_OFF = V_OFF + KV_WIDTH
GATE_OFF = GLU_OFF + 2 * CONV_CHANNELS
IN_WIDTH = GATE_OFF + 2 * D_MODEL
D_FF = 2816
EPS = 1e-5
NEG = -1e30

LANES = 128
SUBLANES = 8
SEQ_TILE = 256
FFN_CHUNK = 256
N_FFN_CHUNKS = D_FF // FFN_CHUNK
FFN_DOWN_GROUPS = ((0, 3), (4, 7), (8, 10))
CONV_ROWS = 128
STAGE_ROWS, STAGE_COLS = 512, 1024
STAGE_SLOTS = 3
VMEM_LIMIT_BYTES = 56 * 1024 * 1024


def _rmsnorm(x, g):
    return x * lax.rsqrt(jnp.mean(x * x, axis=-1, keepdims=True) + EPS) * g


def _dot(a, b):
    return jnp.dot(a, b, preferred_element_type=jnp.float32)


def _dot_nt(a, b):
    return lax.dot_general(a, b, (((1,), (1,)), ((), ())), preferred_element_type=jnp.float32)


def _load_weights(pairs, stage, sem):
    blocks = []
    for src, dst in pairs:
        rows, cols = src.shape
        assert dst.shape == src.shape
        for r0 in range(0, rows, STAGE_ROWS):
            for c0 in range(0, cols, STAGE_COLS):
                blocks.append((src, dst, r0, min(STAGE_ROWS, rows - r0), c0, min(STAGE_COLS, cols - c0)))

    def copy(i):
        src, _, r0, nr, c0, nc = blocks[i]
        slot = i % STAGE_SLOTS
        return pltpu.make_async_copy(src.at[r0:r0 + nr, c0:c0 + nc],
                                     stage.at[slot, 0:nr, 0:nc], sem.at[slot])

    for i in range(min(STAGE_SLOTS - 1, len(blocks))):
        copy(i).start()
    for i, (_, dst, r0, nr, c0, nc) in enumerate(blocks):
        if i + STAGE_SLOTS - 1 < len(blocks):
            copy(i + STAGE_SLOTS - 1).start()
        copy(i).wait()
        dst[r0:r0 + nr, c0:c0 + nc] = stage[i % STAGE_SLOTS, 0:nr, 0:nc].astype(dst.dtype)


def _attn_scores(q, kd_scr):
    T = q.shape[0]
    low_half = lax.broadcasted_iota(jnp.int32, (WINDOW, LANES), 1) < HEAD_DIM
    zero = jnp.zeros((), q.dtype)
    scores = {}
    for b in range(T // WINDOW):
        r0 = b * WINDOW
        for p in range(N_Q_HEADS // 2):
            g = (2 * p) // GROUP
            qp = q[r0:r0 + WINDOW, p * LANES:(p + 1) * LANES]
            rhs = jnp.concatenate([jnp.where(low_half, qp, zero),
                                   jnp.where(low_half, zero, qp)], axis=0)
            scores[b, p] = _dot_nt(kd_scr[g, r0:r0 + 2 * WINDOW, :], rhs)
    return scores


def _attn_probs(scores, sinks_ref, first_tile):
    kj = lax.broadcasted_iota(jnp.int32, (2 * WINDOW, 2 * LANES), 0)
    qi = lax.broadcasted_iota(jnp.int32, (2 * WINDOW, 2 * LANES), 1) % WINDOW
    diff = qi + WINDOW - kj
    band = (diff >= 0) & (diff < WINDOW)
    band_first = band & ((kj >= WINDOW) | jnp.logical_not(first_tile))
    head_lane = lax.broadcasted_iota(jnp.int32, (1, 2 * LANES), 1) < LANES
    probs = {}
    for (b, p), st in scores.items():
        st = jnp.where(band_first if b == 0 else band, st, NEG)
        sink = jnp.where(head_lane, sinks_ref[2 * p], sinks_ref[2 * p + 1])
        m = jnp.maximum(jnp.max(st, axis=0, keepdims=True), sink)
        e = jnp.exp(st - m)
        denom = jnp.sum(e, axis=0, keepdims=True) + jnp.exp(sink - m)
        probs[b, p] = (e.astype(jnp.bfloat16), denom)
    return probs


def _attn_values(probs, vt_scr, T):
    blocks = []
    for b in range(T // WINDOW):
        r0 = b * WINDOW
        pairs = []
        for p in range(N_Q_HEADS // 2):
            g = (2 * p) // GROUP
            e, denom = probs[b, p]
            vt = vt_scr[g * HEAD_DIM:(g + 1) * HEAD_DIM, r0:r0 + 2 * WINDOW]
            ot = _dot(vt, e) / denom
            pairs.append(jnp.concatenate([ot[:, 0:LANES], ot[:, LANES:]], axis=0).T)
        blocks.append(jnp.concatenate(pairs, axis=1))
    return jnp.concatenate(blocks, axis=0)


def _causal_conv(z_scr, convw_ref, convb_ref, T):
    first = CONV_HALO - (CONV_WIDTH - 1)
    col_chunks = []
    for c in range(CONV_CHANNELS // LANES):
        cs = slice(c * LANES, (c + 1) * LANES)
        row_chunks = []
        for r0 in range(0, T, CONV_ROWS):
            acc = jnp.broadcast_to(convb_ref[:, cs], (CONV_ROWS, LANES))
            zwin = z_scr[r0:r0 + CONV_ROWS + CONV_HALO, cs]
            for r in range(SUBLANES):
                rows = CONV_ROWS + (SUBLANES if r else 0)
                part = None
                for o in range(first, first + CONV_WIDTH):
                    if o % SUBLANES != r:
                        continue
                    term = convw_ref[o - first:o - first + 1, cs] * zwin[o - r:o - r + rows, :]
                    part = term if part is None else part + term
                acc = acc + part[r:r + CONV_ROWS, :]
            row_chunks.append(acc)
        col_chunks.append(jnp.concatenate(row_chunks, axis=0))
    return jnp.concatenate(col_chunks, axis=1)


def _layer_kernel(tiles_per_seq, sinks_ref, x_ref, gmix_ref, win_hbm, bin_ref, convw_ref,
                  convb_ref, lng_ref, lnb_ref, wap_hbm, wcp_hbm, bcp_ref, wout_hbm, gffn_ref,
                  wfi_hbm, wfd_hbm, gfin_ref, o_ref, kd_scr, vt_scr, z_scr, x1_scr,
                  win_ref, wap_ref, wcp_ref, wout_ref, wfi_ref, wfd_ref, stage, stage_sem):
    T = SEQ_TILE
    step = pl.program_id(0)
    first_tile = step % tiles_per_seq == 0
    slot = step % 2

    @pl.when(step == 0)
    def _():
        _load_weights([(win_hbm, win_ref), (wap_hbm, wap_ref), (wcp_hbm, wcp_ref),
                       (wout_hbm, wout_ref), (wfi_hbm, wfi_ref), (wfd_hbm, wfd_ref)],
                      stage, stage_sem)
        x1_scr[1] = jnp.zeros((T, D_MODEL), x1_scr.dtype)

    @pl.when(first_tile)
    def _():
        kd_scr[:, 0:WINDOW, :] = jnp.zeros((N_KV_HEADS, WINDOW, LANES), kd_scr.dtype)
        vt_scr[:, 0:WINDOW] = jnp.zeros((KV_WIDTH, WINDOW), vt_scr.dtype)
        z_scr[0:CONV_HALO, :] = jnp.zeros((CONV_HALO, CONV_CHANNELS), z_scr.dtype)

    x1_prev = x1_scr[1 - slot]
    h2 = _rmsnorm(x1_prev, gffn_ref[...]).astype(jnp.bfloat16)
    raw = {}
    acts = {}
    ffn = [x1_prev]

    def ffn_up(j):
        c0 = j * FFN_CHUNK
        raw[j] = (_dot(h2, wfi_ref[:, c0:c0 + FFN_CHUNK]),
                  _dot(h2, wfi_ref[:, D_FF + c0:D_FF + c0 + FFN_CHUNK]))

    def ffn_act(j):
        gate, up = raw.pop(j)
        acts[j] = (gate * jax.nn.sigmoid(gate) * up).astype(jnp.bfloat16)

    def ffn_down(first, last):
        a = jnp.concatenate([acts.pop(j) for j in range(first, last + 1)], axis=1)
        ffn[0] = ffn[0] + _dot(a, wfd_ref[first * FFN_CHUNK:(last + 1) * FFN_CHUNK, :])

    x = x_ref[0]
    h = _rmsnorm(x, gmix_ref[...]).astype(jnp.bfloat16)

    glu = _dot(h, win_ref[:, GLU_OFF:GATE_OFF]) + bin_ref[:, GLU_OFF:GATE_OFF]
    qkv = _dot(h, win_ref[:, 0:GLU_OFF]) + bin_ref[:, 0:GLU_OFF]
    ffn_up(0)

    z_scr[CONV_HALO:CONV_HALO + T, :] = glu[:, 0:CONV_CHANNELS] * jax.nn.sigmoid(glu[:, CONV_CHANNELS:])
    q = (qkv[:, 0:K_OFF] * (HEAD_DIM ** -0.5)).astype(jnp.bfloat16)
    k = qkv[:, K_OFF:V_OFF]
    k_swapped = pltpu.roll(k, HEAD_DIM, 1)
    low_half = lax.broadcasted_iota(jnp.int32, (T, LANES), 1) < HEAD_DIM
    kd_scr[0, WINDOW:WINDOW + T, :] = jnp.where(low_half, k, k_swapped).astype(jnp.bfloat16)
    kd_scr[1, WINDOW:WINDOW + T, :] = jnp.where(low_half, k_swapped, k).astype(jnp.bfloat16)
    vt_scr[:, WINDOW:WINDOW + T] = qkv[:, V_OFF:GLU_OFF].T.astype(jnp.bfloat16)

    scores = _attn_scores(q, kd_scr)
    ffn_up(1)
    gates = jax.nn.sigmoid(_dot(h, win_ref[:, GATE_OFF:IN_WIDTH]) + bin_ref[:, GATE_OFF:IN_WIDTH])
    ffn_up(2)
    probs = _attn_probs(scores, sinks_ref, first_tile)
    attn = _attn_values(probs, vt_scr, T)
    kd_scr[:, 0:WINDOW, :] = kd_scr[:, T:T + WINDOW, :]
    vt_scr[:, 0:WINDOW] = vt_scr[:, T:T + WINDOW]
    ffn_up(3)
    y_attn = _dot(attn.astype(jnp.bfloat16), wap_ref[...])
    for j in range(4, 9):
        ffn_up(j)

    acc = _causal_conv(z_scr, convw_ref, convb_ref, T)
    z_scr[0:CONV_HALO, :] = z_scr[T:T + CONV_HALO, :]
    mu = jnp.mean(acc, axis=-1, keepdims=True)
    xc = acc - mu
    var = jnp.mean(xc * xc, axis=-1, keepdims=True)
    c = xc * lax.rsqrt(var + EPS) * lng_ref[...] + lnb_ref[...]
    c = (c * jax.nn.sigmoid(c)).astype(jnp.bfloat16)
    for j in range(0, 9):
        ffn_act(j)
    ffn_up(9)
    ffn_up(10)
    ffn_act(9)
    ffn_act(10)
    y_conv = _dot(c, wcp_ref[...]) + bcp_ref[...]
    ffn_down(*FFN_DOWN_GROUPS[0])
    ffn_down(*FFN_DOWN_GROUPS[1])
    merged = (gates[:, 0:D_MODEL] * y_attn + gates[:, D_MODEL:] * y_conv).astype(jnp.bfloat16)
    ffn_down(*FFN_DOWN_GROUPS[2])
    x1_scr[slot] = x + _dot(merged, wout_ref[...])

    o_ref[0] = _rmsnorm(ffn[0], gfin_ref[...])


def _resident(shape):
    return pl.BlockSpec(shape, lambda i: (0,) * len(shape), pipeline_mode=pl.Buffered(1))


@jax.jit
def kernel(x, g_mix_norm, w_in, b_in, sinks, conv_w, conv_b, ln_g, ln_b, w_attn_proj,
           w_conv_proj, b_conv_proj, w_out, g_ffn_norm, w_ffn_in, w_ffn_down, g_final):
    B, S, D = x.shape
    assert D == D_MODEL and S % SEQ_TILE == 0 and SEQ_TILE % WINDOW == 0
    assert w_in.shape[0] == 1, "single-layer kernel"
    tiles_per_seq = S // SEQ_TILE
    n_tiles = B * tiles_per_seq
    row = lambda a: a.reshape(1, -1)
    weights = (w_in[0], w_attn_proj[0], w_conv_proj[0], w_out[0], w_ffn_in[0], w_ffn_down[0])
    operands = (
        sinks[0],
        x,
        row(g_mix_norm[0]), weights[0], row(b_in[0]),
        conv_w[0], row(conv_b[0]), row(ln_g[0]), row(ln_b[0]),
        weights[1], weights[2], row(b_conv_proj[0]),
        weights[3], row(g_ffn_norm[0]),
        weights[4], weights[5], row(g_final),
    )

    def tile_block(t):
        return (t // tiles_per_seq, t % tiles_per_seq, 0)

    x_spec = pl.BlockSpec((1, SEQ_TILE, D), lambda i: tile_block(jnp.minimum(i, n_tiles - 1)))
    o_spec = pl.BlockSpec((1, SEQ_TILE, D), lambda i: tile_block(jnp.maximum(i - 1, 0)))
    in_specs = [pl.BlockSpec(memory_space=pltpu.SMEM), x_spec]
    in_specs += [pl.BlockSpec(memory_space=pl.ANY) if any(a is w for w in weights) else _resident(a.shape)
                 for a in operands[2:]]
    return pl.pallas_call(
        functools.partial(_layer_kernel, tiles_per_seq),
        out_shape=jax.ShapeDtypeStruct(x.shape, x.dtype),
        grid=(n_tiles + 1,),
        in_specs=in_specs,
        out_specs=o_spec,
        scratch_shapes=[
            pltpu.VMEM((N_KV_HEADS, WINDOW + SEQ_TILE, LANES), jnp.bfloat16),
            pltpu.VMEM((KV_WIDTH, WINDOW + SEQ_TILE), jnp.bfloat16),
            pltpu.VMEM((SEQ_TILE + CONV_HALO, CONV_CHANNELS), jnp.float32),
            pltpu.VMEM((2, SEQ_TILE, D_MODEL), jnp.float32),
            *[pltpu.VMEM(w.shape, jnp.bfloat16) for w in weights],
            pltpu.VMEM((STAGE_SLOTS, STAGE_ROWS, STAGE_COLS), jnp.float32),
            pltpu.SemaphoreType.DMA((STAGE_SLOTS,)),
        ],
        compiler_params=pltpu.CompilerParams(
            dimension_semantics=("arbitrary",),
            vmem_limit_bytes=VMEM_LIMIT_BYTES,
        ),
        name="hybrid_layer",
    )(*operands)
```

```python
import functools

import jax
import jax.numpy as jnp
from jax import lax
from jax.experimental import pallas as pl
from jax.experimental.pallas import tpu as pltpu

D_MODEL = 1024
HEAD_DIM = 64
N_Q_HEADS = 8
N_KV_HEADS = 2
GROUP = N_Q_HEADS // N_KV_HEADS
WINDOW = 128
ATTN_WIDTH = N_Q_HEADS * HEAD_DIM
KV_WIDTH = N_KV_HEADS * HEAD_DIM
CONV_CHANNELS = 512
CONV_WIDTH = 31
CONV_HALO = 32
K_OFF = ATTN_WIDTH
V_OFF = K_OFF + KV_WIDTH
GLU_OFF = V_OFF + KV_WIDTH
GATE_OFF = GLU_OFF + 2 * CONV_CHANNELS
IN_WIDTH = GATE_OFF + 2 * D_MODEL
D_FF = 2816
EPS = 1e-5
NEG = -1e30

LANES = 128
SUBLANES = 8
SEQ_TILE = 256
FFN_CHUNK = 256
N_FFN_CHUNKS = D_FF // FFN_CHUNK
FFN_DOWN_GROUPS = ((0, 3), (4, 7), (8, 10))
CONV_ROWS = 128
STAGE_ROWS, STAGE_COLS = 512, 1024
STAGE_SLOTS = 3
VMEM_LIMIT_BYTES = 56 * 1024 * 1024


def _rmsnorm(x, g):
    return x * lax.rsqrt(jnp.mean(x * x, axis=-1, keepdims=True) + EPS) * g


def _dot(a, b):
    return jnp.dot(a, b, preferred_element_type=jnp.float32)


def _dot_nt(a, b):
    return lax.dot_general(a, b, (((1,), (1,)), ((), ())), preferred_element_type=jnp.float32)


def _load_weights(pairs, stage, sem):
    blocks = []
    for src, dst in pairs:
        rows, cols = src.shape
        assert dst.shape == src.shape
        for r0 in range(0, rows, STAGE_ROWS):
            for c0 in range(0, cols, STAGE_COLS):
                blocks.append((src, dst, r0, min(STAGE_ROWS, rows - r0), c0, min(STAGE_COLS, cols - c0)))

    def copy(i):
        src, _, r0, nr, c0, nc = blocks[i]
        slot = i % STAGE_SLOTS
        return pltpu.make_async_copy(src.at[r0:r0 + nr, c0:c0 + nc],
                                     stage.at[slot, 0:nr, 0:nc], sem.at[slot])

    for i in range(min(STAGE_SLOTS - 1, len(blocks))):
        copy(i).start()
    for i, (_, dst, r0, nr, c0, nc) in enumerate(blocks):
        if i + STAGE_SLOTS - 1 < len(blocks):
            copy(i + STAGE_SLOTS - 1).start()
        copy(i).wait()
        dst[r0:r0 + nr, c0:c0 + nc] = stage[i % STAGE_SLOTS, 0:nr, 0:nc].astype(dst.dtype)


def _attn_scores(q, kd_scr):
    T = q.shape[0]
    low_half = lax.broadcasted_iota(jnp.int32, (WINDOW, LANES), 1) < HEAD_DIM
    zero = jnp.zeros((), q.dtype)
    scores = {}
    for b in range(T // WINDOW):
        r0 = b * WINDOW
        for p in range(N_Q_HEADS // 2):
            g = (2 * p) // GROUP
            qp = q[r0:r0 + WINDOW, p * LANES:(p + 1) * LANES]
            rhs = jnp.concatenate([jnp.where(low_half, qp, zero),
                                   jnp.where(low_half, zero, qp)], axis=0)
            scores[b, p] = _dot_nt(kd_scr[g, r0:r0 + 2 * WINDOW, :], rhs)
    return scores


def _attn_probs(scores, sinks_ref, first_tile):
    kj = lax.broadcasted_iota(jnp.int32, (2 * WINDOW, 2 * LANES), 0)
    qi = lax.broadcasted_iota(jnp.int32, (2 * WINDOW, 2 * LANES), 1) % WINDOW
    diff = qi + WINDOW - kj
    band = (diff >= 0) & (diff < WINDOW)
    band_first = band & ((kj >= WINDOW) | jnp.logical_not(first_tile))
    head_lane = lax.broadcasted_iota(jnp.int32, (1, 2 * LANES), 1) < LANES
    probs = {}
    for (b, p), st in scores.items():
        st = jnp.where(band_first if b == 0 else band, st, NEG)
        sink = jnp.where(head_lane, sinks_ref[2 * p], sinks_ref[2 * p + 1])
        m = jnp.maximum(jnp.max(st, axis=0, keepdims=True), sink)
        e = jnp.exp(st - m)
        denom = jnp.sum(e, axis=0, keepdims=True) + jnp.exp(sink - m)
        probs[b, p] = (e.astype(jnp.bfloat16), denom)
    return probs


def _attn_values(probs, vt_scr, T):
    blocks = []
    for b in range(T // WINDOW):
        r0 = b * WINDOW
        pairs = []
        for p in range(N_Q_HEADS // 2):
            g = (2 * p) // GROUP
            e, denom = probs[b, p]
            vt = vt_scr[g * HEAD_DIM:(g + 1) * HEAD_DIM, r0:r0 + 2 * WINDOW]
            ot = _dot(vt, e) / denom
            pairs.append(jnp.concatenate([ot[:, 0:LANES], ot[:, LANES:]], axis=0).T)
        blocks.append(jnp.concatenate(pairs, axis=1))
    return jnp.concatenate(blocks, axis=0)


def _causal_conv(z_scr, convw_ref, convb_ref, T):
    first = CONV_HALO - (CONV_WIDTH - 1)
    col_chunks = []
    for c in range(CONV_CHANNELS // LANES):
        cs = slice(c * LANES, (c + 1) * LANES)
        row_chunks = []
        for r0 in range(0, T, CONV_ROWS):
            acc = jnp.broadcast_to(convb_ref[:, cs], (CONV_ROWS, LANES))
            zwin = z_scr[r0:r0 + CONV_ROWS + CONV_HALO, cs]
            for r in range(SUBLANES):
                rows = CONV_ROWS + (SUBLANES if r else 0)
                part = None
                for o in range(first, first + CONV_WIDTH):
                    if o % SUBLANES != r:
                        continue
                    term = convw_ref[o - first:o - first + 1, cs] * zwin[o - r:o - r + rows, :]
                    part = term if part is None else part + term
                acc = acc + part[r:r + CONV_ROWS, :]
            row_chunks.append(acc)
        col_chunks.append(jnp.concatenate(row_chunks, axis=0))
    return jnp.concatenate(col_chunks, axis=1)


def _layer_kernel(tiles_per_seq, n_tiles, sinks_ref, x_ref, gmix_ref, win_hbm, bin_ref, convw_ref,
                  convb_ref, lng_ref, lnb_ref, wap_hbm, wcp_hbm, bcp_ref, wout_hbm, gffn_ref,
                  wfi_hbm, wfd_hbm, gfin_ref, o_ref, kd_scr, vt_scr, z_scr, x1_scr,
                  win_ref, wap_ref, wcp_ref, wout_ref, wfi_ref, wfd_ref, stage, stage_sem):
    T = SEQ_TILE
    step = pl.program_id(0)
    first_tile = step % tiles_per_seq == 0
    slot = step % 2

    @pl.when(step == 0)
    def _():
        _load_weights([(win_hbm, win_ref), (wap_hbm, wap_ref), (wcp_hbm, wcp_ref),
                       (wout_hbm, wout_ref), (wfi_hbm, wfi_ref), (wfd_hbm, wfd_ref)],
                      stage, stage_sem)

    @pl.when(first_tile)
    def _():
        kd_scr[:, 0:WINDOW, :] = jnp.zeros((N_KV_HEADS, WINDOW, LANES), kd_scr.dtype)
        vt_scr[:, 0:WINDOW] = jnp.zeros((KV_WIDTH, WINDOW), vt_scr.dtype)
        z_scr[0:CONV_HALO, :] = jnp.zeros((CONV_HALO, CONV_CHANNELS), z_scr.dtype)

    def body(mixer, swiglu):
        raw, acts, ffn = {}, {}, []
        if swiglu:
            x1_prev = x1_scr[1 - slot]
            h2 = _rmsnorm(x1_prev, gffn_ref[...]).astype(jnp.bfloat16)
            ffn.append(x1_prev)

        def ffn_up(j):
            if swiglu:
                c0 = j * FFN_CHUNK
                raw[j] = (_dot(h2, wfi_ref[:, c0:c0 + FFN_CHUNK]),
                          _dot(h2, wfi_ref[:, D_FF + c0:D_FF + c0 + FFN_CHUNK]))

        def ffn_act(j):
            if swiglu:
                gate, up = raw.pop(j)
                acts[j] = (gate * jax.nn.sigmoid(gate) * up).astype(jnp.bfloat16)

        def ffn_down(first, last):
            if swiglu:
                a = jnp.concatenate([acts.pop(j) for j in range(first, last + 1)], axis=1)
                ffn[0] = ffn[0] + _dot(a, wfd_ref[first * FFN_CHUNK:(last + 1) * FFN_CHUNK, :])

        assert N_FFN_CHUNKS == 11, "the matmul sequence below places chunks 0..10 by hand"
        if mixer:
            x = x_ref[0]
            h = _rmsnorm(x, gmix_ref[...]).astype(jnp.bfloat16)
            glu = _dot(h, win_ref[:, GLU_OFF:GATE_OFF]) + bin_ref[:, GLU_OFF:GATE_OFF]
            qkv = _dot(h, win_ref[:, 0:GLU_OFF]) + bin_ref[:, 0:GLU_OFF]
        ffn_up(0)
        if mixer:
            z_scr[CONV_HALO:CONV_HALO + T, :] = (glu[:, 0:CONV_CHANNELS]
                                                 * jax.nn.sigmoid(glu[:, CONV_CHANNELS:]))
            q = (qkv[:, 0:K_OFF] * (HEAD_DIM ** -0.5)).astype(jnp.bfloat16)
            k = qkv[:, K_OFF:V_OFF]
            k_swapped = pltpu.roll(k, HEAD_DIM, 1)
            low_half = lax.broadcasted_iota(jnp.int32, (T, LANES), 1) < HEAD_DIM
            kd_scr[0, WINDOW:WINDOW + T, :] = jnp.where(low_half, k, k_swapped).astype(jnp.bfloat16)
            kd_scr[1, WINDOW:WINDOW + T, :] = jnp.where(low_half, k_swapped, k).astype(jnp.bfloat16)
            vt_scr[:, WINDOW:WINDOW + T] = qkv[:, V_OFF:GLU_OFF].T.astype(jnp.bfloat16)
            scores = _attn_scores(q, kd_scr)
        ffn_up(1)
        if mixer:
            gates = jax.nn.sigmoid(_dot(h, win_ref[:, GATE_OFF:IN_WIDTH]) + bin_ref[:, GATE_OFF:IN_WIDTH])
        ffn_up(2)
        if mixer:
            probs = _attn_probs(scores, sinks_ref, first_tile)
            attn = _attn_values(probs, vt_scr, T)
            kd_scr[:, 0:WINDOW, :] = kd_scr[:, T:T + WINDOW, :]
            vt_scr[:, 0:WINDOW] = vt_scr[:, T:T + WINDOW]
        ffn_up(3)
        if mixer:
            y_attn = _dot(attn.astype(jnp.bfloat16), wap_ref[...])
        for j in range(4, 9):
            ffn_up(j)
        if mixer:
            acc = _causal_conv(z_scr, convw_ref, convb_ref, T)
            z_scr[0:CONV_HALO, :] = z_scr[T:T + CONV_HALO, :]
            mu = jnp.mean(acc, axis=-1, keepdims=True)
            xc = acc - mu
            var = jnp.mean(xc * xc, axis=-1, keepdims=True)
            c = xc * lax.rsqrt(var + EPS) * lng_ref[...] + lnb_ref[...]
            c = (c * jax.nn.sigmoid(c)).astype(jnp.bfloat16)
        for j in range(0, 9):
            ffn_act(j)
        ffn_up(9)
        ffn_up(10)
        ffn_act(9)
        ffn_act(10)
        if mixer:
            y_conv = _dot(c, wcp_ref[...]) + bcp_ref[...]
        ffn_down(*FFN_DOWN_GROUPS[0])
        ffn_down(*FFN_DOWN_GROUPS[1])
        if mixer:
            merged = (gates[:, 0:D_MODEL] * y_attn + gates[:, D_MODEL:] * y_conv).astype(jnp.bfloat16)
        ffn_down(*FFN_DOWN_GROUPS[2])
        if mixer:
            x1_scr[slot] = x + _dot(merged, wout_ref[...])
        if swiglu:
            o_ref[0] = _rmsnorm(ffn[0], gfin_ref[...])

    pl.when(step == 0)(lambda: body(True, False))
    pl.when((step > 0) & (step < n_tiles))(lambda: body(True, True))
    pl.when(step == n_tiles)(lambda: body(False, True))


def _resident(shape):
    return pl.BlockSpec(shape, lambda i: (0,) * len(shape), pipeline_mode=pl.Buffered(1))


@jax.jit
def kernel(x, g_mix_norm, w_in, b_in, sinks, conv_w, conv_b, ln_g, ln_b, w_attn_proj,
           w_conv_proj, b_conv_proj, w_out, g_ffn_norm, w_ffn_in, w_ffn_down, g_final):
    B, S, D = x.shape
    assert D == D_MODEL and S % SEQ_TILE == 0 and SEQ_TILE % WINDOW == 0
    assert w_in.shape[0] == 1, "single-layer kernel"
    tiles_per_seq = S // SEQ_TILE
    n_tiles = B * tiles_per_seq
    row = lambda a: a.reshape(1, -1)
    weights = (w_in[0], w_attn_proj[0], w_conv_proj[0], w_out[0], w_ffn_in[0], w_ffn_down[0])
    operands = (
        sinks[0],
        x,
        row(g_mix_norm[0]), weights[0], row(b_in[0]),
        conv_w[0], row(conv_b[0]), row(ln_g[0]), row(ln_b[0]),
        weights[1], weights[2], row(b_conv_proj[0]),
        weights[3], row(g_ffn_norm[0]),
        weights[4], weights[5], row(g_final),
    )

    def tile_block(t):
        return (t // tiles_per_seq, t % tiles_per_seq, 0)

    x_spec = pl.BlockSpec((1, SEQ_TILE, D), lambda i: tile_block(jnp.minimum(i, n_tiles - 1)))
    o_spec = pl.BlockSpec((1, SEQ_TILE, D), lambda i: tile_block(jnp.maximum(i - 1, 0)))
    in_specs = [pl.BlockSpec(memory_space=pltpu.SMEM), x_spec]
    in_specs += [pl.BlockSpec(memory_space=pl.ANY) if any(a is w for w in weights) else _resident(a.shape)
                 for a in operands[2:]]
    return pl.pallas_call(
        functools.partial(_layer_kernel, tiles_per_seq, n_tiles),
        out_shape=jax.ShapeDtypeStruct(x.shape, x.dtype),
        grid=(n_tiles + 1,),
        in_specs=in_specs,
        out_specs=o_spec,
        scratch_shapes=[
            pltpu.VMEM((N_KV_HEADS, WINDOW + SEQ_TILE, LANES), jnp.bfloat16),
            pltpu.VMEM((KV_WIDTH, WINDOW + SEQ_TILE), jnp.bfloat16),
            pltpu.VMEM((SEQ_TILE + CONV_HALO, CONV_CHANNELS), jnp.float32),
            pltpu.VMEM((2, SEQ_TILE, D_MODEL), jnp.float32),
            *[pltpu.VMEM(w.shape, jnp.bfloat16) for w in weights],
            pltpu.VMEM((STAGE_SLOTS, STAGE_ROWS, STAGE_COLS), jnp.float32),
            pltpu.SemaphoreType.DMA((STAGE_SLOTS,)),
        ],
        compiler_params=pltpu.CompilerParams(
            dimension_semantics=("arbitrary",),
            vmem_limit_bytes=VMEM_LIMIT_BYTES,
        ),
        name="hybrid_layer",
    )(*operands)
```

```python
import functools

import jax
import jax.numpy as jnp
from jax import lax
from jax.experimental import pallas as pl
from jax.experimental.pallas import tpu as pltpu

D_MODEL = 1024
HEAD_DIM = 64
N_Q_HEADS = 8
N_KV_HEADS = 2
GROUP = N_Q_HEADS // N_KV_HEADS
WINDOW = 128
ATTN_WIDTH = N_Q_HEADS * HEAD_DIM
KV_WIDTH = N_KV_HEADS * HEAD_DIM
CONV_CHANNELS = 512
CONV_WIDTH = 31
CONV_HALO = 32
K_OFF = ATTN_WIDTH
V_OFF = K_OFF + KV_WIDTH
GLU_OFF = V_OFF + KV_WIDTH
GATE_OFF = GLU_OFF + 2 * CONV_CHANNELS
IN_WIDTH = GATE_OFF + 2 * D_MODEL
D_FF = 2816
EPS = 1e-5
NEG = -1e30

LANES = 128
SUBLANES = 8
SEQ_TILE = 256
FFN_CHUNK = 256
N_FFN_CHUNKS = D_FF // FFN_CHUNK
FFN_DOWN_GROUPS = ((0, 3), (4, 7), (8, 10))
CONV_ROWS = 128
STAGE_ROWS, STAGE_COLS = 512, 1024
STAGE_SLOTS = 3
VMEM_LIMIT_BYTES = 56 * 1024 * 1024


def _rmsnorm(x, g):
    return x * lax.rsqrt(jnp.mean(x * x, axis=-1, keepdims=True) + EPS) * g


def _dot(a, b):
    return jnp.dot(a, b, preferred_element_type=jnp.float32)


def _dot_nt(a, b):
    return lax.dot_general(a, b, (((1,), (1,)), ((), ())), preferred_element_type=jnp.float32)


def _load_weights(pairs, stage, sem):
    blocks = []
    for src, dst in pairs:
        rows, cols = src.shape
        assert dst.shape == src.shape
        for r0 in range(0, rows, STAGE_ROWS):
            for c0 in range(0, cols, STAGE_COLS):
                blocks.append((src, dst, r0, min(STAGE_ROWS, rows - r0), c0, min(STAGE_COLS, cols - c0)))

    def copy(i):
        src, _, r0, nr, c0, nc = blocks[i]
        slot = i % STAGE_SLOTS
        return pltpu.make_async_copy(src.at[r0:r0 + nr, c0:c0 + nc],
                                     stage.at[slot, 0:nr, 0:nc], sem.at[slot])

    for i in range(min(STAGE_SLOTS - 1, len(blocks))):
        copy(i).start()
    for i, (_, dst, r0, nr, c0, nc) in enumerate(blocks):
        if i + STAGE_SLOTS - 1 < len(blocks):
            copy(i + STAGE_SLOTS - 1).start()
        copy(i).wait()
        dst[r0:r0 + nr, c0:c0 + nc] = stage[i % STAGE_SLOTS, 0:nr, 0:nc].astype(dst.dtype)


def _attn_scores(q, kd_scr):
    T = q.shape[0]
    low_half = lax.broadcasted_iota(jnp.int32, (WINDOW, LANES), 1) < HEAD_DIM
    zero = jnp.zeros((), q.dtype)
    scores = {}
    for b in range(T // WINDOW):
        r0 = b * WINDOW
        for p in range(N_Q_HEADS // 2):
            g = (2 * p) // GROUP
            qp = q[r0:r0 + WINDOW, p * LANES:(p + 1) * LANES]
            rhs = jnp.concatenate([jnp.where(low_half, qp, zero),
                                   jnp.where(low_half, zero, qp)], axis=0)
            scores[b, p] = _dot_nt(kd_scr[g, r0:r0 + 2 * WINDOW, :], rhs)
    return scores


def _attn_probs(scores, sinks_ref, first_tile):
    kj = lax.broadcasted_iota(jnp.int32, (2 * WINDOW, 2 * LANES), 0)
    qi = lax.broadcasted_iota(jnp.int32, (2 * WINDOW, 2 * LANES), 1) % WINDOW
    diff = qi + WINDOW - kj
    band = (diff >= 0) & (diff < WINDOW)
    band_first = band & ((kj >= WINDOW) | jnp.logical_not(first_tile))
    head_lane = lax.broadcasted_iota(jnp.int32, (1, 2 * LANES), 1) < LANES
    probs = {}
    for (b, p), st in scores.items():
        st = jnp.where(band_first if b == 0 else band, st, NEG)
        sink = jnp.where(head_lane, sinks_ref[2 * p], sinks_ref[2 * p + 1])
        m = jnp.maximum(jnp.max(st, axis=0, keepdims=True), sink)
        e = jnp.exp(st - m)
        denom = jnp.sum(e, axis=0, keepdims=True) + jnp.exp(sink - m)
        probs[b, p] = (e.astype(jnp.bfloat16), denom)
    return probs


def _attn_values(probs, vt_scr, T):
    blocks = []
    for b in range(T // WINDOW):
        r0 = b * WINDOW
        pairs = []
        for p in range(N_Q_HEADS // 2):
            g = (2 * p) // GROUP
            e, denom = probs[b, p]
            vt = vt_scr[g * HEAD_DIM:(g + 1) * HEAD_DIM, r0:r0 + 2 * WINDOW]
            ot = _dot(vt, e) / denom
            pairs.append(jnp.concatenate([ot[:, 0:LANES], ot[:, LANES:]], axis=0).T)
        blocks.append(jnp.concatenate(pairs, axis=1))
    return jnp.concatenate(blocks, axis=0)


def _causal_conv(z_scr, convw_ref, convb_ref, T):
    first = CONV_HALO - (CONV_WIDTH - 1)
    col_chunks = []
    for c in range(CONV_CHANNELS // LANES):
        cs = slice(c * LANES, (c + 1) * LANES)
        row_chunks = []
        for r0 in range(0, T, CONV_ROWS):
            acc = jnp.broadcast_to(convb_ref[:, cs], (CONV_ROWS, LANES))
            zwin = z_scr[r0:r0 + CONV_ROWS + CONV_HALO, cs]
            for r in range(SUBLANES):
                rows = CONV_ROWS + (SUBLANES if r else 0)
                part = None
                for o in range(first, first + CONV_WIDTH):
                    if o % SUBLANES != r:
                        continue
                    term = convw_ref[o - first:o - first + 1, cs] * zwin[o - r:o - r + rows, :]
                    part = term if part is None else part + term
                acc = acc + part[r:r + CONV_ROWS, :]
            row_chunks.append(acc)
        col_chunks.append(jnp.concatenate(row_chunks, axis=0))
    return jnp.concatenate(col_chunks, axis=1)


def _layer_kernel(tiles_per_seq, n_tiles, sinks_ref, x_ref, gmix_ref, win_hbm, bin_ref, convw_ref,
                  convb_ref, lng_ref, lnb_ref, wap_hbm, wcp_hbm, bcp_ref, wout_hbm, gffn_ref,
                  wfi_hbm, wfd_hbm, gfin_ref, o_ref, kd_scr, vt_scr, z_scr, x1_scr,
                  win_ref, wap_ref, wcp_ref, wout_ref, wfi_ref, wfd_ref, stage, stage_sem):
    T = SEQ_TILE
    step = pl.program_id(0)
    first_tile = step % tiles_per_seq == 0
    slot = step % 2

    @pl.when(step == 0)
    def _():
        _load_weights([(win_hbm, win_ref), (wap_hbm, wap_ref), (wcp_hbm, wcp_ref),
                       (wout_hbm, wout_ref), (wfi_hbm, wfi_ref), (wfd_hbm, wfd_ref)],
                      stage, stage_sem)

    @pl.when(first_tile)
    def _():
        kd_scr[:, 0:WINDOW, :] = jnp.zeros((N_KV_HEADS, WINDOW, LANES), kd_scr.dtype)
        vt_scr[:, 0:WINDOW] = jnp.zeros((KV_WIDTH, WINDOW), vt_scr.dtype)
        z_scr[0:CONV_HALO, :] = jnp.zeros((CONV_HALO, CONV_CHANNELS), z_scr.dtype)

    def body(mixer, swiglu):
        acts, ffn = {}, []
        if swiglu:
            x1_prev = x1_scr[1 - slot]
            h2 = _rmsnorm(x1_prev, gffn_ref[...]).astype(jnp.bfloat16)
            ffn.append(x1_prev)

        def ffn_up(j):
            if swiglu:
                c0 = j * FFN_CHUNK
                gate = _dot(h2, wfi_ref[:, c0:c0 + FFN_CHUNK])
                up = _dot(h2, wfi_ref[:, D_FF + c0:D_FF + c0 + FFN_CHUNK])
                acts[j] = (gate * jax.nn.sigmoid(gate) * up).astype(jnp.bfloat16)

        def ffn_down(first, last):
            if swiglu:
                a = jnp.concatenate([acts.pop(j) for j in range(first, last + 1)], axis=1)
                ffn[0] = ffn[0] + _dot(a, wfd_ref[first * FFN_CHUNK:(last + 1) * FFN_CHUNK, :])

        assert N_FFN_CHUNKS == 11, "the matmul sequence below places chunks 0..10 by hand"
        if mixer:
            x = x_ref[0]
            h = _rmsnorm(x, gmix_ref[...]).astype(jnp.bfloat16)
            glu = _dot(h, win_ref[:, GLU_OFF:GATE_OFF]) + bin_ref[:, GLU_OFF:GATE_OFF]
            qkv = _dot(h, win_ref[:, 0:GLU_OFF]) + bin_ref[:, 0:GLU_OFF]
        ffn_up(0)
        if mixer:
            z_scr[CONV_HALO:CONV_HALO + T, :] = (glu[:, 0:CONV_CHANNELS]
                                                 * jax.nn.sigmoid(glu[:, CONV_CHANNELS:]))
            q = (qkv[:, 0:K_OFF] * (HEAD_DIM ** -0.5)).astype(jnp.bfloat16)
            k = qkv[:, K_OFF:V_OFF]
            k_swapped = pltpu.roll(k, HEAD_DIM, 1)
            low_half = lax.broadcasted_iota(jnp.int32, (T, LANES), 1) < HEAD_DIM
            kd_scr[0, WINDOW:WINDOW + T, :] = jnp.where(low_half, k, k_swapped).astype(jnp.bfloat16)
            kd_scr[1, WINDOW:WINDOW + T, :] = jnp.where(low_half, k_swapped, k).astype(jnp.bfloat16)
            vt_scr[:, WINDOW:WINDOW + T] = qkv[:, V_OFF:GLU_OFF].T.astype(jnp.bfloat16)
            scores = _attn_scores(q, kd_scr)
        ffn_up(1)
        if mixer:
            gates = jax.nn.sigmoid(_dot(h, win_ref[:, GATE_OFF:IN_WIDTH]) + bin_ref[:, GATE_OFF:IN_WIDTH])
        ffn_up(2)
        if mixer:
            probs = _attn_probs(scores, sinks_ref, first_tile)
            attn = _attn_values(probs, vt_scr, T)
            kd_scr[:, 0:WINDOW, :] = kd_scr[:, T:T + WINDOW, :]
            vt_scr[:, 0:WINDOW] = vt_scr[:, T:T + WINDOW]
        ffn_up(3)
        if mixer:
            y_attn = _dot(attn.astype(jnp.bfloat16), wap_ref[...])
        for j in range(4, N_FFN_CHUNKS):
            ffn_up(j)
        if mixer:
            acc = _causal_conv(z_scr, convw_ref, convb_ref, T)
            z_scr[0:CONV_HALO, :] = z_scr[T:T + CONV_HALO, :]
            mu = jnp.mean(acc, axis=-1, keepdims=True)
            xc = acc - mu
            var = jnp.mean(xc * xc, axis=-1, keepdims=True)
            c = xc * lax.rsqrt(var + EPS) * lng_ref[...] + lnb_ref[...]
            c = (c * jax.nn.sigmoid(c)).astype(jnp.bfloat16)
            y_conv = _dot(c, wcp_ref[...]) + bcp_ref[...]
        ffn_down(*FFN_DOWN_GROUPS[0])
        ffn_down(*FFN_DOWN_GROUPS[1])
        if mixer:
            merged = (gates[:, 0:D_MODEL] * y_attn + gates[:, D_MODEL:] * y_conv).astype(jnp.bfloat16)
        ffn_down(*FFN_DOWN_GROUPS[2])
        if mixer:
            x1_scr[slot] = x + _dot(merged, wout_ref[...])
        if swiglu:
            o_ref[0] = _rmsnorm(ffn[0], gfin_ref[...])

    pl.when(step == 0)(lambda: body(True, False))
    pl.when((step > 0) & (step < n_tiles))(lambda: body(True, True))
    pl.when(step == n_tiles)(lambda: body(False, True))


def _resident(shape):
    return pl.BlockSpec(shape, lambda i: (0,) * len(shape), pipeline_mode=pl.Buffered(1))


@jax.jit
def kernel(x, g_mix_norm, w_in, b_in, sinks, conv_w, conv_b, ln_g, ln_b, w_attn_proj,
           w_conv_proj, b_conv_proj, w_out, g_ffn_norm, w_ffn_in, w_ffn_down, g_final):
    B, S, D = x.shape
    assert D == D_MODEL and S % SEQ_TILE == 0 and SEQ_TILE % WINDOW == 0
    assert w_in.shape[0] == 1, "single-layer kernel"
    tiles_per_seq = S // SEQ_TILE
    n_tiles = B * tiles_per_seq
    row = lambda a: a.reshape(1, -1)
    weights = (w_in[0], w_attn_proj[0], w_conv_proj[0], w_out[0], w_ffn_in[0], w_ffn_down[0])
    operands = (
        sinks[0],
        x,
        row(g_mix_norm[0]), weights[0], row(b_in[0]),
        conv_w[0], row(conv_b[0]), row(ln_g[0]), row(ln_b[0]),
        weights[1], weights[2], row(b_conv_proj[0]),
        weights[3], row(g_ffn_norm[0]),
        weights[4], weights[5], row(g_final),
    )

    def tile_block(t):
        return (t // tiles_per_seq, t % tiles_per_seq, 0)

    x_spec = pl.BlockSpec((1, SEQ_TILE, D), lambda i: tile_block(jnp.minimum(i, n_tiles - 1)))
    o_spec = pl.BlockSpec((1, SEQ_TILE, D), lambda i: tile_block(jnp.maximum(i - 1, 0)))
    in_specs = [pl.BlockSpec(memory_space=pltpu.SMEM), x_spec]
    in_specs += [pl.BlockSpec(memory_space=pl.ANY) if any(a is w for w in weights) else _resident(a.shape)
                 for a in operands[2:]]
    return pl.pallas_call(
        functools.partial(_layer_kernel, tiles_per_seq, n_tiles),
        out_shape=jax.ShapeDtypeStruct(x.shape, x.dtype),
        grid=(n_tiles + 1,),
        in_specs=in_specs,
        out_specs=o_spec,
        scratch_shapes=[
            pltpu.VMEM((N_KV_HEADS, WINDOW + SEQ_TILE, LANES), jnp.bfloat16),
            pltpu.VMEM((KV_WIDTH, WINDOW + SEQ_TILE), jnp.bfloat16),
            pltpu.VMEM((SEQ_TILE + CONV_HALO, CONV_CHANNELS), jnp.float32),
            pltpu.VMEM((2, SEQ_TILE, D_MODEL), jnp.float32),
            *[pltpu.VMEM(w.shape, jnp.bfloat16) for w in weights],
            pltpu.VMEM((STAGE_SLOTS, STAGE_ROWS, STAGE_COLS), jnp.float32),
            pltpu.SemaphoreType.DMA((STAGE_SLOTS,)),
        ],
        compiler_params=pltpu.CompilerParams(
            dimension_semantics=("arbitrary",),
            vmem_limit_bytes=VMEM_LIMIT_BYTES,
        ),
        name="hybrid_layer",
    )(*operands)
```

```python
import functools

import jax
import jax.numpy as jnp
from jax import lax
from jax.experimental import pallas as pl
from jax.experimental.pallas import tpu as pltpu

D_MODEL = 1024
HEAD_DIM = 64
N_Q_HEADS = 8
N_KV_HEADS = 2
GROUP = N_Q_HEADS // N_KV_HEADS
WINDOW = 128
ATTN_WIDTH = N_Q_HEADS * HEAD_DIM
KV_WIDTH = N_KV_HEADS * HEAD_DIM
CONV_CHANNELS = 512
CONV_WIDTH = 31
CONV_HALO = 32
K_OFF = ATTN_WIDTH
V_OFF = K_OFF + KV_WIDTH
GLU_OFF = V_OFF + KV_WIDTH
GATE_OFF = GLU_OFF + 2 * CONV_CHANNELS
IN_WIDTH = GATE_OFF + 2 * D_MODEL
D_FF = 2816
EPS = 1e-5
NEG = -1e30

LANES = 128
SUBLANES = 8
SEQ_TILE = 256
FFN_CHUNK = 256
N_FFN_CHUNKS = D_FF // FFN_CHUNK
FFN_DOWN_GROUPS = ((0, 3), (4, 7), (8, 10))
CONV_ROWS = 128
STAGE_ROWS, STAGE_COLS = 512, 1024
STAGE_SLOTS = 3
VMEM_LIMIT_BYTES = 56 * 1024 * 1024


def _rmsnorm(x, g):
    return x * lax.rsqrt(jnp.mean(x * x, axis=-1, keepdims=True) + EPS) * g


def _dot(a, b):
    return jnp.dot(a, b, preferred_element_type=jnp.float32)


def _dot_nt(a, b):
    return lax.dot_general(a, b, (((1,), (1,)), ((), ())), preferred_element_type=jnp.float32)


def _load_weights(pairs, stage, sem):
    blocks = []
    for src, dst in pairs:
        rows, cols = src.shape
        assert dst.shape == src.shape
        for r0 in range(0, rows, STAGE_ROWS):
            for c0 in range(0, cols, STAGE_COLS):
                blocks.append((src, dst, r0, min(STAGE_ROWS, rows - r0), c0, min(STAGE_COLS, cols - c0)))

    def copy(i):
        src, _, r0, nr, c0, nc = blocks[i]
        slot = i % STAGE_SLOTS
        return pltpu.make_async_copy(src.at[r0:r0 + nr, c0:c0 + nc],
                                     stage.at[slot, 0:nr, 0:nc], sem.at[slot])

    for i in range(min(STAGE_SLOTS - 1, len(blocks))):
        copy(i).start()
    for i, (_, dst, r0, nr, c0, nc) in enumerate(blocks):
        if i + STAGE_SLOTS - 1 < len(blocks):
            copy(i + STAGE_SLOTS - 1).start()
        copy(i).wait()
        dst[r0:r0 + nr, c0:c0 + nc] = stage[i % STAGE_SLOTS, 0:nr, 0:nc].astype(dst.dtype)


def _attn_scores(q, kd_scr):
    T = q.shape[0]
    low_half = lax.broadcasted_iota(jnp.int32, (WINDOW, LANES), 1) < HEAD_DIM
    zero = jnp.zeros((), q.dtype)
    scores = {}
    for b in range(T // WINDOW):
        r0 = b * WINDOW
        for p in range(N_Q_HEADS // 2):
            g = (2 * p) // GROUP
            qp = q[r0:r0 + WINDOW, p * LANES:(p + 1) * LANES]
            rhs = jnp.concatenate([jnp.where(low_half, qp, zero),
                                   jnp.where(low_half, zero, qp)], axis=0)
            scores[b, p] = _dot_nt(kd_scr[g, r0:r0 + 2 * WINDOW, :], rhs)
    return scores


def _attn_probs(scores, sinks_ref, first_tile):
    kj = lax.broadcasted_iota(jnp.int32, (2 * WINDOW, 2 * LANES), 0)
    qi = lax.broadcasted_iota(jnp.int32, (2 * WINDOW, 2 * LANES), 1) % WINDOW
    diff = qi + WINDOW - kj
    band = (diff >= 0) & (diff < WINDOW)
    band_first = band & ((kj >= WINDOW) | jnp.logical_not(first_tile))
    head_lane = lax.broadcasted_iota(jnp.int32, (1, 2 * LANES), 1) < LANES
    probs = {}
    for (b, p), st in scores.items():
        st = jnp.where(band_first if b == 0 else band, st, NEG)
        sink = jnp.where(head_lane, sinks_ref[2 * p], sinks_ref[2 * p + 1])
        m = jnp.maximum(jnp.max(st, axis=0, keepdims=True), sink)
        e = jnp.exp(st - m)
        denom = jnp.sum(e, axis=0, keepdims=True) + jnp.exp(sink - m)
        probs[b, p] = (e.astype(jnp.bfloat16), denom)
    return probs


def _attn_values(probs, vt_scr, T):
    blocks = []
    for b in range(T // WINDOW):
        r0 = b * WINDOW
        pairs = []
        for p in range(N_Q_HEADS // 2):
            g = (2 * p) // GROUP
            e, denom = probs[b, p]
            vt = vt_scr[g * HEAD_DIM:(g + 1) * HEAD_DIM, r0:r0 + 2 * WINDOW]
            ot = _dot(vt, e) / denom
            pairs.append(jnp.concatenate([ot[:, 0:LANES], ot[:, LANES:]], axis=0).T)
        blocks.append(jnp.concatenate(pairs, axis=1))
    return jnp.concatenate(blocks, axis=0)


def _causal_conv(z_scr, convw_ref, convb_ref, T):
    first = CONV_HALO - (CONV_WIDTH - 1)
    col_chunks = []
    for c in range(CONV_CHANNELS // LANES):
        cs = slice(c * LANES, (c + 1) * LANES)
        row_chunks = []
        for r0 in range(0, T, CONV_ROWS):
            acc = jnp.broadcast_to(convb_ref[:, cs], (CONV_ROWS, LANES))
            zwin = z_scr[r0:r0 + CONV_ROWS + CONV_HALO, cs]
            zb = (zwin.astype(jnp.bfloat16), zwin[SUBLANES:, :].astype(jnp.bfloat16))
            for r in range(SUBLANES):
                rows = CONV_ROWS + (SUBLANES if r else 0)
                part = None
                for o in range(first, first + CONV_WIDTH):
                    if o % SUBLANES != r:
                        continue
                    a = (o - r) // SUBLANES
                    start = (a - a % 2) * SUBLANES
                    term = (convw_ref[o - first:o - first + 1, cs].astype(jnp.bfloat16)
                            * zb[a % 2][start:start + rows, :])
                    part = term if part is None else part + term
                acc = acc + part.astype(jnp.float32)[r:r + CONV_ROWS, :]
            row_chunks.append(acc)
        col_chunks.append(jnp.concatenate(row_chunks, axis=0))
    return jnp.concatenate(col_chunks, axis=1)


def _layer_kernel(tiles_per_seq, n_tiles, sinks_ref, x_ref, gmix_ref, win_hbm, bin_ref, convw_ref,
                  convb_ref, lng_ref, lnb_ref, wap_hbm, wcp_hbm, bcp_ref, wout_hbm, gffn_ref,
                  wfi_hbm, wfd_hbm, gfin_ref, o_ref, kd_scr, vt_scr, z_scr, x1_scr,
                  win_ref, wap_ref, wcp_ref, wout_ref, wfi_ref, wfd_ref, stage, stage_sem):
    T = SEQ_TILE
    step = pl.program_id(0)
    first_tile = step % tiles_per_seq == 0
    slot = step % 2

    @pl.when(step == 0)
    def _():
        _load_weights([(win_hbm, win_ref), (wap_hbm, wap_ref), (wcp_hbm, wcp_ref),
                       (wout_hbm, wout_ref), (wfi_hbm, wfi_ref), (wfd_hbm, wfd_ref)],
                      stage, stage_sem)

    @pl.when(first_tile)
    def _():
        kd_scr[:, 0:WINDOW, :] = jnp.zeros((N_KV_HEADS, WINDOW, LANES), kd_scr.dtype)
        vt_scr[:, 0:WINDOW] = jnp.zeros((KV_WIDTH, WINDOW), vt_scr.dtype)
        z_scr[0:CONV_HALO, :] = jnp.zeros((CONV_HALO, CONV_CHANNELS), z_scr.dtype)

    def body(mixer, swiglu):
        acts, ffn = {}, []
        if swiglu:
            x1_prev = x1_scr[1 - slot]
            h2 = _rmsnorm(x1_prev, gffn_ref[...]).astype(jnp.bfloat16)
            ffn.append(x1_prev)

        def ffn_up(j):
            if swiglu:
                c0 = j * FFN_CHUNK
                gate = _dot(h2, wfi_ref[:, c0:c0 + FFN_CHUNK])
                up = _dot(h2, wfi_ref[:, D_FF + c0:D_FF + c0 + FFN_CHUNK])
                acts[j] = (gate * jax.nn.sigmoid(gate) * up).astype(jnp.bfloat16)

        def ffn_down(first, last):
            if swiglu:
                a = jnp.concatenate([acts.pop(j) for j in range(first, last + 1)], axis=1)
                ffn[0] = ffn[0] + _dot(a, wfd_ref[first * FFN_CHUNK:(last + 1) * FFN_CHUNK, :])

        assert N_FFN_CHUNKS == 11, "the matmul sequence below places chunks 0..10 by hand"
        if mixer:
            x = x_ref[0]
            h = _rmsnorm(x, gmix_ref[...]).astype(jnp.bfloat16)
            glu = _dot(h, win_ref[:, GLU_OFF:GATE_OFF]) + bin_ref[:, GLU_OFF:GATE_OFF]
            qkv = _dot(h, win_ref[:, 0:GLU_OFF]) + bin_ref[:, 0:GLU_OFF]
        ffn_up(0)
        if mixer:
            z_scr[CONV_HALO:CONV_HALO + T, :] = (glu[:, 0:CONV_CHANNELS]
                                                 * jax.nn.sigmoid(glu[:, CONV_CHANNELS:]))
            q = (qkv[:, 0:K_OFF] * (HEAD_DIM ** -0.5)).astype(jnp.bfloat16)
            k = qkv[:, K_OFF:V_OFF]
            k_swapped = pltpu.roll(k, HEAD_DIM, 1)
            low_half = lax.broadcasted_iota(jnp.int32, (T, LANES), 1) < HEAD_DIM
            kd_scr[0, WINDOW:WINDOW + T, :] = jnp.where(low_half, k, k_swapped).astype(jnp.bfloat16)
            kd_scr[1, WINDOW:WINDOW + T, :] = jnp.where(low_half, k_swapped, k).astype(jnp.bfloat16)
            vt_scr[:, WINDOW:WINDOW + T] = qkv[:, V_OFF:GLU_OFF].T.astype(jnp.bfloat16)
            scores = _attn_scores(q, kd_scr)
        ffn_up(1)
        if mixer:
            gates = jax.nn.sigmoid(_dot(h, win_ref[:, GATE_OFF:IN_WIDTH]) + bin_ref[:, GATE_OFF:IN_WIDTH])
        ffn_up(2)
        if mixer:
            probs = _attn_probs(scores, sinks_ref, first_tile)
            attn = _attn_values(probs, vt_scr, T)
            kd_scr[:, 0:WINDOW, :] = kd_scr[:, T:T + WINDOW, :]
            vt_scr[:, 0:WINDOW] = vt_scr[:, T:T + WINDOW]
        ffn_up(3)
        if mixer:
            y_attn = _dot(attn.astype(jnp.bfloat16), wap_ref[...])
        for j in range(4, N_FFN_CHUNKS):
            ffn_up(j)
        if mixer:
            acc = _causal_conv(z_scr, convw_ref, convb_ref, T)
            z_scr[0:CONV_HALO, :] = z_scr[T:T + CONV_HALO, :]
            mu = jnp.mean(acc, axis=-1, keepdims=True)
            xc = acc - mu
            var = jnp.mean(xc * xc, axis=-1, keepdims=True)
            c = xc * lax.rsqrt(var + EPS) * lng_ref[...] + lnb_ref[...]
            c = (c * jax.nn.sigmoid(c)).astype(jnp.bfloat16)
            y_conv = _dot(c, wcp_ref[...]) + bcp_ref[...]
        ffn_down(*FFN_DOWN_GROUPS[0])
        ffn_down(*FFN_DOWN_GROUPS[1])
        if mixer:
            merged = (gates[:, 0:D_MODEL] * y_attn + gates[:, D_MODEL:] * y_conv).astype(jnp.bfloat16)
        ffn_down(*FFN_DOWN_GROUPS[2])
        if mixer:
            x1_scr[slot] = x + _dot(merged, wout_ref[...])
        if swiglu:
            o_ref[0] = _rmsnorm(ffn[0], gfin_ref[...])

    pl.when(step == 0)(lambda: body(True, False))
    pl.when((step > 0) & (step < n_tiles))(lambda: body(True, True))
    pl.when(step == n_tiles)(lambda: body(False, True))


def _resident(shape):
    return pl.BlockSpec(shape, lambda i: (0,) * len(shape), pipeline_mode=pl.Buffered(1))


@jax.jit
def kernel(x, g_mix_norm, w_in, b_in, sinks, conv_w, conv_b, ln_g, ln_b, w_attn_proj,
           w_conv_proj, b_conv_proj, w_out, g_ffn_norm, w_ffn_in, w_ffn_down, g_final):
    B, S, D = x.shape
    assert D == D_MODEL and S % SEQ_TILE == 0 and SEQ_TILE % WINDOW == 0
    assert w_in.shape[0] == 1, "single-layer kernel"
    tiles_per_seq = S // SEQ_TILE
    n_tiles = B * tiles_per_seq
    row = lambda a: a.reshape(1, -1)
    weights = (w_in[0], w_attn_proj[0], w_conv_proj[0], w_out[0], w_ffn_in[0], w_ffn_down[0])
    operands = (
        sinks[0],
        x,
        row(g_mix_norm[0]), weights[0], row(b_in[0]),
        conv_w[0], row(conv_b[0]), row(ln_g[0]), row(ln_b[0]),
        weights[1], weights[2], row(b_conv_proj[0]),
        weights[3], row(g_ffn_norm[0]),
        weights[4], weights[5], row(g_final),
    )

    def tile_block(t):
        return (t // tiles_per_seq, t % tiles_per_seq, 0)

    x_spec = pl.BlockSpec((1, SEQ_TILE, D), lambda i: tile_block(jnp.minimum(i, n_tiles - 1)))
    o_spec = pl.BlockSpec((1, SEQ_TILE, D), lambda i: tile_block(jnp.maximum(i - 1, 0)))
    in_specs = [pl.BlockSpec(memory_space=pltpu.SMEM), x_spec]
    in_specs += [pl.BlockSpec(memory_space=pl.ANY) if any(a is w for w in weights) else _resident(a.shape)
                 for a in operands[2:]]
    return pl.pallas_call(
        functools.partial(_layer_kernel, tiles_per_seq, n_tiles),
        out_shape=jax.ShapeDtypeStruct(x.shape, x.dtype),
        grid=(n_tiles + 1,),
        in_specs=in_specs,
        out_specs=o_spec,
        scratch_shapes=[
            pltpu.VMEM((N_KV_HEADS, WINDOW + SEQ_TILE, LANES), jnp.bfloat16),
            pltpu.VMEM((KV_WIDTH, WINDOW + SEQ_TILE), jnp.bfloat16),
            pltpu.VMEM((SEQ_TILE + CONV_HALO, CONV_CHANNELS), jnp.float32),
            pltpu.VMEM((2, SEQ_TILE, D_MODEL), jnp.float32),
            *[pltpu.VMEM(w.shape, jnp.bfloat16) for w in weights],
            pltpu.VMEM((STAGE_SLOTS, STAGE_ROWS, STAGE_COLS), jnp.float32),
            pltpu.SemaphoreType.DMA((STAGE_SLOTS,)),
        ],
        compiler_params=pltpu.CompilerParams(
            dimension_semantics=("arbitrary",),
            vmem_limit_bytes=VMEM_LIMIT_BYTES,
        ),
        name="hybrid_layer",
    )(*operands)
```

```python
import functools

import jax
import jax.numpy as jnp
from jax import lax
from jax.experimental import pallas as pl
from jax.experimental.pallas import tpu as pltpu

D_MODEL = 1024
HEAD_DIM = 64
N_Q_HEADS = 8
N_KV_HEADS = 2
GROUP = N_Q_HEADS // N_KV_HEADS
WINDOW = 128
ATTN_WIDTH = N_Q_HEADS * HEAD_DIM
KV_WIDTH = N_KV_HEADS * HEAD_DIM
CONV_CHANNELS = 512
CONV_WIDTH = 31
CONV_HALO = 32
K_OFF = ATTN_WIDTH
V_OFF = K_OFF + KV_WIDTH
GLU_OFF = V_OFF + KV_WIDTH
GATE_OFF = GLU_OFF + 2 * CONV_CHANNELS
IN_WIDTH = GATE_OFF + 2 * D_MODEL
D_FF = 2816
EPS = 1e-5
NEG = -1e30
LOG2E = 1.4426950408889634

LANES = 128
SUBLANES = 8
SEQ_TILE = 256
FFN_CHUNK = 256
N_FFN_CHUNKS = D_FF // FFN_CHUNK
FFN_DOWN_GROUPS = ((0, 3), (4, 7), (8, 10))
CONV_ROWS = 128
STAGE_ROWS, STAGE_COLS = 512, 1024
STAGE_SLOTS = 3
VMEM_LIMIT_BYTES = 56 * 1024 * 1024


def _rmsnorm(x, g):
    return x * lax.rsqrt(jnp.mean(x * x, axis=-1, keepdims=True) + EPS) * g


def _dot(a, b):
    return jnp.dot(a, b, preferred_element_type=jnp.float32)


def _dot_nt(a, b):
    return lax.dot_general(a, b, (((1,), (1,)), ((), ())), preferred_element_type=jnp.float32)


def _load_weights(pairs, stage, sem):
    blocks = []
    for src, dst in pairs:
        rows, cols = src.shape
        assert dst.shape == src.shape
        for r0 in range(0, rows, STAGE_ROWS):
            for c0 in range(0, cols, STAGE_COLS):
                blocks.append((src, dst, r0, min(STAGE_ROWS, rows - r0), c0, min(STAGE_COLS, cols - c0)))

    def copy(i):
        src, _, r0, nr, c0, nc = blocks[i]
        slot = i % STAGE_SLOTS
        return pltpu.make_async_copy(src.at[r0:r0 + nr, c0:c0 + nc],
                                     stage.at[slot, 0:nr, 0:nc], sem.at[slot])

    for i in range(min(STAGE_SLOTS - 1, len(blocks))):
        copy(i).start()
    for i, (_, dst, r0, nr, c0, nc) in enumerate(blocks):
        if i + STAGE_SLOTS - 1 < len(blocks):
            copy(i + STAGE_SLOTS - 1).start()
        copy(i).wait()
        dst[r0:r0 + nr, c0:c0 + nc] = stage[i % STAGE_SLOTS, 0:nr, 0:nc].astype(dst.dtype)


def _attn_scores(q, kd_scr):
    T = q.shape[0]
    low_half = lax.broadcasted_iota(jnp.int32, (WINDOW, LANES), 1) < HEAD_DIM
    zero = jnp.zeros((), q.dtype)
    scores = {}
    for b in range(T // WINDOW):
        r0 = b * WINDOW
        for p in range(N_Q_HEADS // 2):
            g = (2 * p) // GROUP
            qp = q[r0:r0 + WINDOW, p * LANES:(p + 1) * LANES]
            rhs = jnp.concatenate([jnp.where(low_half, qp, zero),
                                   jnp.where(low_half, zero, qp)], axis=0)
            scores[b, p] = _dot_nt(kd_scr[g, r0:r0 + 2 * WINDOW, :], rhs)
    return scores


def _attn_probs(scores, sinks_ref, first_tile):
    kj = lax.broadcasted_iota(jnp.int32, (2 * WINDOW, 2 * LANES), 0)
    qi = lax.broadcasted_iota(jnp.int32, (2 * WINDOW, 2 * LANES), 1) % WINDOW
    diff = qi + WINDOW - kj
    band = (diff >= 0) & (diff < WINDOW)
    band_first = band & ((kj >= WINDOW) | jnp.logical_not(first_tile))
    head_lane = lax.broadcasted_iota(jnp.int32, (1, 2 * LANES), 1) < LANES
    probs = {}
    for (b, p), st in scores.items():
        st = jnp.where(band_first if b == 0 else band, st, NEG)
        sink = jnp.where(head_lane, sinks_ref[2 * p], sinks_ref[2 * p + 1]) * LOG2E
        m = jnp.maximum(jnp.max(st, axis=0, keepdims=True), sink)
        e = jnp.exp2(st - m)
        denom = jnp.sum(e, axis=0, keepdims=True) + jnp.exp2(sink - m)
        probs[b, p] = (e.astype(jnp.bfloat16), denom)
    return probs


def _attn_values(probs, vt_scr, T):
    blocks = []
    for b in range(T // WINDOW):
        r0 = b * WINDOW
        pairs = []
        for p in range(N_Q_HEADS // 2):
            g = (2 * p) // GROUP
            e, denom = probs[b, p]
            vt = vt_scr[g * HEAD_DIM:(g + 1) * HEAD_DIM, r0:r0 + 2 * WINDOW]
            ot = _dot(vt, e) / denom
            pairs.append(jnp.concatenate([ot[:, 0:LANES], ot[:, LANES:]], axis=0).T)
        blocks.append(jnp.concatenate(pairs, axis=1))
    return jnp.concatenate(blocks, axis=0)


def _causal_conv(z_scr, convw_ref, convb_ref, T):
    first = CONV_HALO - (CONV_WIDTH - 1)
    col_chunks = []
    for c in range(CONV_CHANNELS // LANES):
        cs = slice(c * LANES, (c + 1) * LANES)
        row_chunks = []
        for r0 in range(0, T, CONV_ROWS):
            acc = jnp.broadcast_to(convb_ref[:, cs], (CONV_ROWS, LANES))
            zwin = z_scr[r0:r0 + CONV_ROWS + CONV_HALO, cs]
            zb = (zwin.astype(jnp.bfloat16), zwin[SUBLANES:, :].astype(jnp.bfloat16))
            for r in range(SUBLANES):
                rows = CONV_ROWS + (SUBLANES if r else 0)
                part = None
                for o in range(first, first + CONV_WIDTH):
                    if o % SUBLANES != r:
                        continue
                    a = (o - r) // SUBLANES
                    start = (a - a % 2) * SUBLANES
                    term = (convw_ref[o - first:o - first + 1, cs].astype(jnp.bfloat16)
                            * zb[a % 2][start:start + rows, :])
                    part = term if part is None else part + term
                acc = acc + part.astype(jnp.float32)[r:r + CONV_ROWS, :]
            row_chunks.append(acc)
        col_chunks.append(jnp.concatenate(row_chunks, axis=0))
    return jnp.concatenate(col_chunks, axis=1)


def _layer_kernel(tiles_per_seq, n_tiles, sinks_ref, x_ref, gmix_ref, win_hbm, bin_ref, convw_ref,
                  convb_ref, lng_ref, lnb_ref, wap_hbm, wcp_hbm, bcp_ref, wout_hbm, gffn_ref,
                  wfi_hbm, wfd_hbm, gfin_ref, o_ref, kd_scr, vt_scr, z_scr, x1_scr,
                  win_ref, wap_ref, wcp_ref, wout_ref, wfi_ref, wfd_ref, stage, stage_sem):
    T = SEQ_TILE
    step = pl.program_id(0)
    first_tile = step % tiles_per_seq == 0
    slot = step % 2

    @pl.when(step == 0)
    def _():
        _load_weights([(win_hbm, win_ref), (wap_hbm, wap_ref), (wcp_hbm, wcp_ref),
                       (wout_hbm, wout_ref), (wfi_hbm, wfi_ref), (wfd_hbm, wfd_ref)],
                      stage, stage_sem)

    @pl.when(first_tile)
    def _():
        kd_scr[:, 0:WINDOW, :] = jnp.zeros((N_KV_HEADS, WINDOW, LANES), kd_scr.dtype)
        vt_scr[:, 0:WINDOW] = jnp.zeros((KV_WIDTH, WINDOW), vt_scr.dtype)
        z_scr[0:CONV_HALO, :] = jnp.zeros((CONV_HALO, CONV_CHANNELS), z_scr.dtype)

    def body(mixer, swiglu):
        acts, ffn = {}, []
        if swiglu:
            x1_prev = x1_scr[1 - slot]
            h2 = _rmsnorm(x1_prev, gffn_ref[...]).astype(jnp.bfloat16)
            ffn.append(x1_prev)

        def ffn_up(j):
            if swiglu:
                c0 = j * FFN_CHUNK
                gate = _dot(h2, wfi_ref[:, c0:c0 + FFN_CHUNK])
                up = _dot(h2, wfi_ref[:, D_FF + c0:D_FF + c0 + FFN_CHUNK])
                acts[j] = (gate * jax.nn.sigmoid(gate) * up).astype(jnp.bfloat16)

        def ffn_down(first, last):
            if swiglu:
                a = jnp.concatenate([acts.pop(j) for j in range(first, last + 1)], axis=1)
                ffn[0] = ffn[0] + _dot(a, wfd_ref[first * FFN_CHUNK:(last + 1) * FFN_CHUNK, :])

        assert N_FFN_CHUNKS == 11, "the matmul sequence below places chunks 0..10 by hand"
        if mixer:
            x = x_ref[0]
            h = _rmsnorm(x, gmix_ref[...]).astype(jnp.bfloat16)
            glu = _dot(h, win_ref[:, GLU_OFF:GATE_OFF]) + bin_ref[:, GLU_OFF:GATE_OFF]
            qkv = _dot(h, win_ref[:, 0:GLU_OFF]) + bin_ref[:, 0:GLU_OFF]
        ffn_up(0)
        if mixer:
            z_scr[CONV_HALO:CONV_HALO + T, :] = (glu[:, 0:CONV_CHANNELS]
                                                 * jax.nn.sigmoid(glu[:, CONV_CHANNELS:]))
            q = qkv[:, 0:K_OFF].astype(jnp.bfloat16)
            k = qkv[:, K_OFF:V_OFF] * (HEAD_DIM ** -0.5 * LOG2E)
            k_swapped = pltpu.roll(k, HEAD_DIM, 1)
            low_half = lax.broadcasted_iota(jnp.int32, (T, LANES), 1) < HEAD_DIM
            kd_scr[0, WINDOW:WINDOW + T, :] = jnp.where(low_half, k, k_swapped).astype(jnp.bfloat16)
            kd_scr[1, WINDOW:WINDOW + T, :] = jnp.where(low_half, k_swapped, k).astype(jnp.bfloat16)
            vt_scr[:, WINDOW:WINDOW + T] = qkv[:, V_OFF:GLU_OFF].T.astype(jnp.bfloat16)
            scores = _attn_scores(q, kd_scr)
        ffn_up(1)
        if mixer:
            gates = jax.nn.sigmoid(_dot(h, win_ref[:, GATE_OFF:IN_WIDTH]) + bin_ref[:, GATE_OFF:IN_WIDTH])
        ffn_up(2)
        if mixer:
            probs = _attn_probs(scores, sinks_ref, first_tile)
            attn = _attn_values(probs, vt_scr, T)
            kd_scr[:, 0:WINDOW, :] = kd_scr[:, T:T + WINDOW, :]
            vt_scr[:, 0:WINDOW] = vt_scr[:, T:T + WINDOW]
        ffn_up(3)
        if mixer:
            y_attn = _dot(attn.astype(jnp.bfloat16), wap_ref[...])
        for j in range(4, N_FFN_CHUNKS):
            ffn_up(j)
        if mixer:
            acc = _causal_conv(z_scr, convw_ref, convb_ref, T)
            z_scr[0:CONV_HALO, :] = z_scr[T:T + CONV_HALO, :]
            mu = jnp.mean(acc, axis=-1, keepdims=True)
            xc = acc - mu
            var = jnp.mean(xc * xc, axis=-1, keepdims=True)
            c = xc * lax.rsqrt(var + EPS) * lng_ref[...] + lnb_ref[...]
            c = (c * jax.nn.sigmoid(c)).astype(jnp.bfloat16)
            y_conv = _dot(c, wcp_ref[...]) + bcp_ref[...]
        ffn_down(*FFN_DOWN_GROUPS[0])
        ffn_down(*FFN_DOWN_GROUPS[1])
        if mixer:
            merged = (gates[:, 0:D_MODEL] * y_attn + gates[:, D_MODEL:] * y_conv).astype(jnp.bfloat16)
        ffn_down(*FFN_DOWN_GROUPS[2])
        if mixer:
            x1_scr[slot] = x + _dot(merged, wout_ref[...])
        if swiglu:
            o_ref[0] = _rmsnorm(ffn[0], gfin_ref[...])

    pl.when(step == 0)(lambda: body(True, False))
    pl.when((step > 0) & (step < n_tiles))(lambda: body(True, True))
    pl.when(step == n_tiles)(lambda: body(False, True))


def _resident(shape):
    return pl.BlockSpec(shape, lambda i: (0,) * len(shape), pipeline_mode=pl.Buffered(1))


@jax.jit
def kernel(x, g_mix_norm, w_in, b_in, sinks, conv_w, conv_b, ln_g, ln_b, w_attn_proj,
           w_conv_proj, b_conv_proj, w_out, g_ffn_norm, w_ffn_in, w_ffn_down, g_final):
    B, S, D = x.shape
    assert D == D_MODEL and S % SEQ_TILE == 0 and SEQ_TILE % WINDOW == 0
    assert w_in.shape[0] == 1, "single-layer kernel"
    tiles_per_seq = S // SEQ_TILE
    n_tiles = B * tiles_per_seq
    row = lambda a: a.reshape(1, -1)
    weights = (w_in[0], w_attn_proj[0], w_conv_proj[0], w_out[0], w_ffn_in[0], w_ffn_down[0])
    operands = (
        sinks[0],
        x,
        row(g_mix_norm[0]), weights[0], row(b_in[0]),
        conv_w[0], row(conv_b[0]), row(ln_g[0]), row(ln_b[0]),
        weights[1], weights[2], row(b_conv_proj[0]),
        weights[3], row(g_ffn_norm[0]),
        weights[4], weights[5], row(g_final),
    )

    def tile_block(t):
        return (t // tiles_per_seq, t % tiles_per_seq, 0)

    x_spec = pl.BlockSpec((1, SEQ_TILE, D), lambda i: tile_block(jnp.minimum(i, n_tiles - 1)))
    o_spec = pl.BlockSpec((1, SEQ_TILE, D), lambda i: tile_block(jnp.maximum(i - 1, 0)))
    in_specs = [pl.BlockSpec(memory_space=pltpu.SMEM), x_spec]
    in_specs += [pl.BlockSpec(memory_space=pl.ANY) if any(a is w for w in weights) else _resident(a.shape)
                 for a in operands[2:]]
    return pl.pallas_call(
        functools.partial(_layer_kernel, tiles_per_seq, n_tiles),
        out_shape=jax.ShapeDtypeStruct(x.shape, x.dtype),
        grid=(n_tiles + 1,),
        in_specs=in_specs,
        out_specs=o_spec,
        scratch_shapes=[
            pltpu.VMEM((N_KV_HEADS, WINDOW + SEQ_TILE, LANES), jnp.bfloat16),
            pltpu.VMEM((KV_WIDTH, WINDOW + SEQ_TILE), jnp.bfloat16),
            pltpu.VMEM((SEQ_TILE + CONV_HALO, CONV_CHANNELS), jnp.float32),
            pltpu.VMEM((2, SEQ_TILE, D_MODEL), jnp.float32),
            *[pltpu.VMEM(w.shape, jnp.bfloat16) for w in weights],
            pltpu.VMEM((STAGE_SLOTS, STAGE_ROWS, STAGE_COLS), jnp.float32),
            pltpu.SemaphoreType.DMA((STAGE_SLOTS,)),
        ],
        compiler_params=pltpu.CompilerParams(
            dimension_semantics=("arbitrary",),
            vmem_limit_bytes=VMEM_LIMIT_BYTES,
        ),
        name="hybrid_layer",
    )(*operands)
```

```python
import functools

import jax
import jax.numpy as jnp
from jax import lax
from jax.experimental import pallas as pl
from jax.experimental.pallas import tpu as pltpu

D_MODEL = 1024
HEAD_DIM = 64
N_Q_HEADS = 8
N_KV_HEADS = 2
GROUP = N_Q_HEADS // N_KV_HEADS
WINDOW = 128
ATTN_WIDTH = N_Q_HEADS * HEAD_DIM
KV_WIDTH = N_KV_HEADS * HEAD_DIM
CONV_CHANNELS = 512
CONV_WIDTH = 31
CONV_HALO = 32
K_OFF = ATTN_WIDTH
V_OFF = K_OFF + KV_WIDTH
GLU_OFF = V_OFF + KV_WIDTH
GATE_OFF = GLU_OFF + 2 * CONV_CHANNELS
IN_WIDTH = GATE_OFF + 2 * D_MODEL
D_FF = 2816
EPS = 1e-5
NEG = -1e30
LOG2E = 1.4426950408889634

LANES = 128
SUBLANES = 8
SEQ_TILE = 256
FFN_CHUNK = 256
N_FFN_CHUNKS = D_FF // FFN_CHUNK
FFN_DOWN_GROUPS = ((0, 3), (4, 7), (8, 10))
CONV_ROWS = 128
STAGE_ROWS, STAGE_COLS = 512, 1024
STAGE_SLOTS = 3
VMEM_LIMIT_BYTES = 56 * 1024 * 1024


def _rmsnorm(x, g):
    return x * lax.rsqrt(jnp.mean(x * x, axis=-1, keepdims=True) + EPS) * g


def _dot(a, b):
    return jnp.dot(a, b, preferred_element_type=jnp.float32)


def _dot_nt(a, b):
    return lax.dot_general(a, b, (((1,), (1,)), ((), ())), preferred_element_type=jnp.float32)


def _load_weights(pairs, stage, sem):
    blocks = []
    for src, dst in pairs:
        rows, cols = src.shape
        assert dst.shape == src.shape
        for r0 in range(0, rows, STAGE_ROWS):
            for c0 in range(0, cols, STAGE_COLS):
                blocks.append((src, dst, r0, min(STAGE_ROWS, rows - r0), c0, min(STAGE_COLS, cols - c0)))

    def copy(i):
        src, _, r0, nr, c0, nc = blocks[i]
        slot = i % STAGE_SLOTS
        return pltpu.make_async_copy(src.at[r0:r0 + nr, c0:c0 + nc],
                                     stage.at[slot, 0:nr, 0:nc], sem.at[slot])

    for i in range(min(STAGE_SLOTS - 1, len(blocks))):
        copy(i).start()
    for i, (_, dst, r0, nr, c0, nc) in enumerate(blocks):
        if i + STAGE_SLOTS - 1 < len(blocks):
            copy(i + STAGE_SLOTS - 1).start()
        copy(i).wait()
        dst[r0:r0 + nr, c0:c0 + nc] = stage[i % STAGE_SLOTS, 0:nr, 0:nc].astype(dst.dtype)


def _attn_scores(q, kd_scr):
    T = q.shape[0]
    low_half = lax.broadcasted_iota(jnp.int32, (WINDOW, LANES), 1) < HEAD_DIM
    zero = jnp.zeros((), q.dtype)
    scores = {}
    for b in range(T // WINDOW):
        r0 = b * WINDOW
        for p in range(N_Q_HEADS // 2):
            g = (2 * p) // GROUP
            qp = q[r0:r0 + WINDOW, p * LANES:(p + 1) * LANES]
            rhs = jnp.concatenate([jnp.where(low_half, qp, zero),
                                   jnp.where(low_half, zero, qp)], axis=0)
            scores[b, p] = _dot_nt(kd_scr[g, r0:r0 + 2 * WINDOW, :], rhs)
    return scores


def _attn_probs(scores, sinks_ref, first_tile):
    kj = lax.broadcasted_iota(jnp.int32, (2 * WINDOW, 2 * LANES), 0)
    qi = lax.broadcasted_iota(jnp.int32, (2 * WINDOW, 2 * LANES), 1) % WINDOW
    diff = qi + WINDOW - kj
    band = (diff >= 0) & (diff < WINDOW)
    band_first = band & ((kj >= WINDOW) | jnp.logical_not(first_tile))
    head_lane = lax.broadcasted_iota(jnp.int32, (1, 2 * LANES), 1) < LANES
    probs = {}
    for (b, p), st in scores.items():
        st = jnp.where(band_first if b == 0 else band, st, NEG)
        sink = jnp.where(head_lane, sinks_ref[2 * p], sinks_ref[2 * p + 1]) * LOG2E
        m = jnp.maximum(jnp.max(st, axis=0, keepdims=True), sink)
        e = jnp.exp2(st - m)
        denom = jnp.sum(e, axis=0, keepdims=True) + jnp.exp2(sink - m)
        probs[b, p] = (e.astype(jnp.bfloat16), denom)
    return probs


def _attn_values(probs, vt_scr, T):
    blocks = []
    for b in range(T // WINDOW):
        r0 = b * WINDOW
        pairs = []
        for p in range(N_Q_HEADS // 2):
            g = (2 * p) // GROUP
            e, denom = probs[b, p]
            vt = vt_scr[g * HEAD_DIM:(g + 1) * HEAD_DIM, r0:r0 + 2 * WINDOW]
            ot = _dot(vt, e) / denom
            pairs.append(jnp.concatenate([ot[:, 0:LANES], ot[:, LANES:]], axis=0).T)
        blocks.append(jnp.concatenate(pairs, axis=1))
    return jnp.concatenate(blocks, axis=0)


def _causal_conv(z_scr, convw_ref, convb_ref, T):
    first = CONV_HALO - (CONV_WIDTH - 1)
    col_chunks = []
    for c in range(CONV_CHANNELS // LANES):
        cs = slice(c * LANES, (c + 1) * LANES)
        row_chunks = []
        for r0 in range(0, T, CONV_ROWS):
            acc = jnp.broadcast_to(convb_ref[:, cs], (CONV_ROWS, LANES))
            zwin = z_scr[r0:r0 + CONV_ROWS + CONV_HALO, cs]
            zb = (zwin.astype(jnp.bfloat16), zwin[SUBLANES:, :].astype(jnp.bfloat16))
            for r in range(SUBLANES):
                rows = CONV_ROWS + (SUBLANES if r else 0)
                part = None
                for o in range(first, first + CONV_WIDTH):
                    if o % SUBLANES != r:
                        continue
                    a = (o - r) // SUBLANES
                    start = (a - a % 2) * SUBLANES
                    term = (convw_ref[o - first:o - first + 1, cs].astype(jnp.bfloat16)
                            * zb[a % 2][start:start + rows, :])
                    part = term if part is None else part + term
                acc = acc + part.astype(jnp.float32)[r:r + CONV_ROWS, :]
            row_chunks.append(acc)
        col_chunks.append(jnp.concatenate(row_chunks, axis=0))
    return jnp.concatenate(col_chunks, axis=1)


def _layer_kernel(tiles_per_seq, n_tiles, sinks_ref, x_ref, gmix_ref, win_hbm, bin_ref, convw_ref,
                  convb_ref, lng_ref, lnb_ref, wap_hbm, wcp_hbm, bcp_ref, wout_hbm, gffn_ref,
                  wfi_hbm, wfd_hbm, gfin_ref, o_ref, kd_scr, vt_scr, z_scr, x1_scr,
                  win_ref, wap_ref, wcp_ref, wout_ref, wfi_ref, wfd_ref, stage, stage_sem):
    T = SEQ_TILE
    step = pl.program_id(0)
    first_tile = step % tiles_per_seq == 0
    slot = step % 2

    @pl.when(step == 0)
    def _():
        _load_weights([(win_hbm, win_ref), (wap_hbm, wap_ref), (wcp_hbm, wcp_ref),
                       (wout_hbm, wout_ref), (wfi_hbm, wfi_ref), (wfd_hbm, wfd_ref)],
                      stage, stage_sem)

    @pl.when(first_tile)
    def _():
        kd_scr[:, 0:WINDOW, :] = jnp.zeros((N_KV_HEADS, WINDOW, LANES), kd_scr.dtype)
        vt_scr[:, 0:WINDOW] = jnp.zeros((KV_WIDTH, WINDOW), vt_scr.dtype)
        z_scr[0:CONV_HALO, :] = jnp.zeros((CONV_HALO, CONV_CHANNELS), z_scr.dtype)

    def body(mixer, swiglu):
        acts, ffn = {}, []
        if swiglu:
            x1_prev = x1_scr[1 - slot]
            h2 = _rmsnorm(x1_prev, gffn_ref[...]).astype(jnp.bfloat16)
            ffn.append(x1_prev)

        def ffn_up(j):
            if swiglu:
                c0 = j * FFN_CHUNK
                gate = _dot(h2, wfi_ref[:, c0:c0 + FFN_CHUNK])
                up = _dot(h2, wfi_ref[:, D_FF + c0:D_FF + c0 + FFN_CHUNK])
                acts[j] = (gate * jax.nn.sigmoid(gate) * up).astype(jnp.bfloat16)

        def ffn_down(first, last):
            if swiglu:
                a = jnp.concatenate([acts.pop(j) for j in range(first, last + 1)], axis=1)
                ffn[0] = ffn[0] + _dot(a, wfd_ref[first * FFN_CHUNK:(last + 1) * FFN_CHUNK, :])

        assert N_FFN_CHUNKS == 11, "the matmul sequence below places chunks 0..10 by hand"
        if mixer:
            x = x_ref[0]
            h = _rmsnorm(x, gmix_ref[...]).astype(jnp.bfloat16)
            glu = _dot(h, win_ref[:, GLU_OFF:GATE_OFF]) + bin_ref[:, GLU_OFF:GATE_OFF]
            qkv = _dot(h, win_ref[:, 0:GLU_OFF]) + bin_ref[:, 0:GLU_OFF]
        ffn_up(0)
        if mixer:
            z_scr[CONV_HALO:CONV_HALO + T, :] = (glu[:, 0:CONV_CHANNELS]
                                                 * jax.nn.sigmoid(glu[:, CONV_CHANNELS:]))
            q = qkv[:, 0:K_OFF].astype(jnp.bfloat16)
            k = qkv[:, K_OFF:V_OFF] * (HEAD_DIM ** -0.5 * LOG2E)
            k_swapped = pltpu.roll(k, HEAD_DIM, 1)
            low_half = lax.broadcasted_iota(jnp.int32, (T, LANES), 1) < HEAD_DIM
            kd_scr[0, WINDOW:WINDOW + T, :] = jnp.where(low_half, k, k_swapped).astype(jnp.bfloat16)
            kd_scr[1, WINDOW:WINDOW + T, :] = jnp.where(low_half, k_swapped, k).astype(jnp.bfloat16)
            vt_scr[:, WINDOW:WINDOW + T] = qkv[:, V_OFF:GLU_OFF].T.astype(jnp.bfloat16)
            scores = _attn_scores(q, kd_scr)
            gates = jax.nn.sigmoid(_dot(h, win_ref[:, GATE_OFF:IN_WIDTH]) + bin_ref[:, GATE_OFF:IN_WIDTH])
        ffn_up(1)
        if mixer:
            probs = _attn_probs(scores, sinks_ref, first_tile)
            attn = _attn_values(probs, vt_scr, T)
            kd_scr[:, 0:WINDOW, :] = kd_scr[:, T:T + WINDOW, :]
            vt_scr[:, 0:WINDOW] = vt_scr[:, T:T + WINDOW]
        ffn_up(2)
        if mixer:
            y_attn = _dot(attn.astype(jnp.bfloat16), wap_ref[...])
        for j in range(3, N_FFN_CHUNKS):
            ffn_up(j)
        if mixer:
            acc = _causal_conv(z_scr, convw_ref, convb_ref, T)
            z_scr[0:CONV_HALO, :] = z_scr[T:T + CONV_HALO, :]
            mu = jnp.mean(acc, axis=-1, keepdims=True)
            xc = acc - mu
            var = jnp.mean(xc * xc, axis=-1, keepdims=True)
            c = xc * lax.rsqrt(var + EPS) * lng_ref[...] + lnb_ref[...]
            c = (c * jax.nn.sigmoid(c)).astype(jnp.bfloat16)
            y_conv = _dot(c, wcp_ref[...]) + bcp_ref[...]
        ffn_down(*FFN_DOWN_GROUPS[0])
        ffn_down(*FFN_DOWN_GROUPS[1])
        if mixer:
            merged = (gates[:, 0:D_MODEL] * y_attn + gates[:, D_MODEL:] * y_conv).astype(jnp.bfloat16)
        ffn_down(*FFN_DOWN_GROUPS[2])
        if mixer:
            x1_scr[slot] = x + _dot(merged, wout_ref[...])
        if swiglu:
            o_ref[0] = _rmsnorm(ffn[0], gfin_ref[...])

    pl.when(step == 0)(lambda: body(True, False))
    pl.when((step > 0) & (step < n_tiles))(lambda: body(True, True))
    pl.when(step == n_tiles)(lambda: body(False, True))


def _resident(shape):
    return pl.BlockSpec(shape, lambda i: (0,) * len(shape), pipeline_mode=pl.Buffered(1))


@jax.jit
def kernel(x, g_mix_norm, w_in, b_in, sinks, conv_w, conv_b, ln_g, ln_b, w_attn_proj,
           w_conv_proj, b_conv_proj, w_out, g_ffn_norm, w_ffn_in, w_ffn_down, g_final):
    B, S, D = x.shape
    assert D == D_MODEL and S % SEQ_TILE == 0 and SEQ_TILE % WINDOW == 0
    assert w_in.shape[0] == 1, "single-layer kernel"
    tiles_per_seq = S // SEQ_TILE
    n_tiles = B * tiles_per_seq
    row = lambda a: a.reshape(1, -1)
    weights = (w_in[0], w_attn_proj[0], w_conv_proj[0], w_out[0], w_ffn_in[0], w_ffn_down[0])
    operands = (
        sinks[0],
        x,
        row(g_mix_norm[0]), weights[0], row(b_in[0]),
        conv_w[0], row(conv_b[0]), row(ln_g[0]), row(ln_b[0]),
        weights[1], weights[2], row(b_conv_proj[0]),
        weights[3], row(g_ffn_norm[0]),
        weights[4], weights[5], row(g_final),
    )

    def tile_block(t):
        return (t // tiles_per_seq, t % tiles_per_seq, 0)

    x_spec = pl.BlockSpec((1, SEQ_TILE, D), lambda i: tile_block(jnp.minimum(i, n_tiles - 1)))
    o_spec = pl.BlockSpec((1, SEQ_TILE, D), lambda i: tile_block(jnp.maximum(i - 1, 0)))
    in_specs = [pl.BlockSpec(memory_space=pltpu.SMEM), x_spec]
    in_specs += [pl.BlockSpec(memory_space=pl.ANY) if any(a is w for w in weights) else _resident(a.shape)
                 for a in operands[2:]]
    return pl.pallas_call(
        functools.partial(_layer_kernel, tiles_per_seq, n_tiles),
        out_shape=jax.ShapeDtypeStruct(x.shape, x.dtype),
        grid=(n_tiles + 1,),
        in_specs=in_specs,
        out_specs=o_spec,
        scratch_shapes=[
            pltpu.VMEM((N_KV_HEADS, WINDOW + SEQ_TILE, LANES), jnp.bfloat16),
            pltpu.VMEM((KV_WIDTH, WINDOW + SEQ_TILE), jnp.bfloat16),
            pltpu.VMEM((SEQ_TILE + CONV_HALO, CONV_CHANNELS), jnp.float32),
            pltpu.VMEM((2, SEQ_TILE, D_MODEL), jnp.float32),
            *[pltpu.VMEM(w.shape, jnp.bfloat16) for w in weights],
            pltpu.VMEM((STAGE_SLOTS, STAGE_ROWS, STAGE_COLS), jnp.float32),
            pltpu.SemaphoreType.DMA((STAGE_SLOTS,)),
        ],
        compiler_params=pltpu.CompilerParams(
            dimension_semantics=("arbitrary",),
            vmem_limit_bytes=VMEM_LIMIT_BYTES,
        ),
        name="hybrid_layer",
    )(*operands)
```

```python
import functools

import jax
import jax.numpy as jnp
from jax import lax
from jax.experimental import pallas as pl
from jax.experimental.pallas import tpu as pltpu

D_MODEL = 1024
HEAD_DIM = 64
N_Q_HEADS = 8
N_KV_HEADS = 2
GROUP = N_Q_HEADS // N_KV_HEADS
WINDOW = 128
ATTN_WIDTH = N_Q_HEADS * HEAD_DIM
KV_WIDTH = N_KV_HEADS * HEAD_DIM
CONV_CHANNELS = 512
CONV_WIDTH = 31
CONV_HALO = 32
K_OFF = ATTN_WIDTH
V_OFF = K_OFF + KV_WIDTH
GLU_OFF = V_OFF + KV_WIDTH
GATE_OFF = GLU_OFF + 2 * CONV_CHANNELS
IN_WIDTH = GATE_OFF + 2 * D_MODEL
D_FF = 2816
EPS = 1e-5
NEG = -1e30
LOG2E = 1.4426950408889634

LANES = 128
SUBLANES = 8
SEQ_TILE = 256
FFN_CHUNK = 256
N_FFN_CHUNKS = D_FF // FFN_CHUNK
FFN_DOWN_GROUPS = ((0, 3), (4, 7), (8, 10))
CONV_ROWS = 128
STAGE_ROWS, STAGE_COLS = 512, 1024
STAGE_SLOTS = 3
VMEM_LIMIT_BYTES = 56 * 1024 * 1024


def _rmsnorm(x, g):
    return x * lax.rsqrt(jnp.mean(x * x, axis=-1, keepdims=True) + EPS) * g


def _dot(a, b):
    return jnp.dot(a, b, preferred_element_type=jnp.float32)


def _dot_nt(a, b):
    return lax.dot_general(a, b, (((1,), (1,)), ((), ())), preferred_element_type=jnp.float32)


def _load_weights(pairs, stage, sem):
    blocks = []
    for src, dst in pairs:
        rows, cols = src.shape
        assert dst.shape == src.shape
        for r0 in range(0, rows, STAGE_ROWS):
            for c0 in range(0, cols, STAGE_COLS):
                blocks.append((src, dst, r0, min(STAGE_ROWS, rows - r0), c0, min(STAGE_COLS, cols - c0)))

    def copy(i):
        src, _, r0, nr, c0, nc = blocks[i]
        slot = i % STAGE_SLOTS
        return pltpu.make_async_copy(src.at[r0:r0 + nr, c0:c0 + nc],
                                     stage.at[slot, 0:nr, 0:nc], sem.at[slot])

    for i in range(min(STAGE_SLOTS - 1, len(blocks))):
        copy(i).start()
    for i, (_, dst, r0, nr, c0, nc) in enumerate(blocks):
        if i + STAGE_SLOTS - 1 < len(blocks):
            copy(i + STAGE_SLOTS - 1).start()
        copy(i).wait()
        dst[r0:r0 + nr, c0:c0 + nc] = stage[i % STAGE_SLOTS, 0:nr, 0:nc].astype(dst.dtype)


def _attn_scores(q, kd_scr):
    T = q.shape[0]
    low_half = lax.broadcasted_iota(jnp.int32, (WINDOW, LANES), 1) < HEAD_DIM
    zero = jnp.zeros((), q.dtype)
    scores = {}
    for b in range(T // WINDOW):
        r0 = b * WINDOW
        for p in range(N_Q_HEADS // 2):
            g = (2 * p) // GROUP
            qp = q[r0:r0 + WINDOW, p * LANES:(p + 1) * LANES]
            rhs = jnp.concatenate([jnp.where(low_half, qp, zero),
                                   jnp.where(low_half, zero, qp)], axis=0)
            scores[b, p] = _dot_nt(kd_scr[g, r0:r0 + 2 * WINDOW, :], rhs)
    return scores


def _attn_probs(scores, sinks_ref, first_tile):
    kj = lax.broadcasted_iota(jnp.int32, (2 * WINDOW, 2 * LANES), 0)
    qi = lax.broadcasted_iota(jnp.int32, (2 * WINDOW, 2 * LANES), 1) % WINDOW
    diff = qi + WINDOW - kj
    band = (diff >= 0) & (diff < WINDOW)
    band_first = band & ((kj >= WINDOW) | jnp.logical_not(first_tile))
    head_lane = lax.broadcasted_iota(jnp.int32, (1, 2 * LANES), 1) < LANES
    probs = {}
    for (b, p), st in scores.items():
        st = jnp.where(band_first if b == 0 else band, st, NEG)
        sink = jnp.where(head_lane, sinks_ref[2 * p], sinks_ref[2 * p + 1]) * LOG2E
        m = jnp.maximum(jnp.max(st, axis=0, keepdims=True), sink)
        e = jnp.exp2(st - m)
        denom = jnp.sum(e, axis=0, keepdims=True) + jnp.exp2(sink - m)
        probs[b, p] = (e.astype(jnp.bfloat16), denom)
    return probs


def _attn_values(probs, vt_scr, T):
    blocks = []
    for b in range(T // WINDOW):
        r0 = b * WINDOW
        pairs = []
        for p in range(N_Q_HEADS // 2):
            g = (2 * p) // GROUP
            e, denom = probs[b, p]
            vt = vt_scr[g * HEAD_DIM:(g + 1) * HEAD_DIM, r0:r0 + 2 * WINDOW]
            ot = _dot(vt, e) / denom
            pairs.append(jnp.concatenate([ot[:, 0:LANES], ot[:, LANES:]], axis=0).T)
        blocks.append(jnp.concatenate(pairs, axis=1))
    return jnp.concatenate(blocks, axis=0)


def _causal_conv(z_scr, convw_ref, convb_ref, T):
    first = CONV_HALO - (CONV_WIDTH - 1)
    col_chunks = []
    for c in range(CONV_CHANNELS // LANES):
        cs = slice(c * LANES, (c + 1) * LANES)
        row_chunks = []
        for r0 in range(0, T, CONV_ROWS):
            acc = jnp.broadcast_to(convb_ref[:, cs], (CONV_ROWS, LANES))
            zwin = z_scr[r0:r0 + CONV_ROWS + CONV_HALO, cs]
            zb = (zwin.astype(jnp.bfloat16), zwin[SUBLANES:, :].astype(jnp.bfloat16))
            for r in range(SUBLANES):
                rows = CONV_ROWS + (SUBLANES if r else 0)
                part = None
                for o in range(first, first + CONV_WIDTH):
                    if o % SUBLANES != r:
                        continue
                    a = (o - r) // SUBLANES
                    start = (a - a % 2) * SUBLANES
                    term = (convw_ref[o - first:o - first + 1, cs].astype(jnp.bfloat16)
                            * zb[a % 2][start:start + rows, :])
                    part = term if part is None else part + term
                acc = acc + part.astype(jnp.float32)[r:r + CONV_ROWS, :]
            row_chunks.append(acc)
        col_chunks.append(jnp.concatenate(row_chunks, axis=0))
    return jnp.concatenate(col_chunks, axis=1)


def _layer_kernel(tiles_per_seq, n_tiles, sinks_ref, x_ref, gmix_ref, win_hbm, bin_ref, convw_ref,
                  convb_ref, lng_ref, lnb_ref, wap_hbm, wcp_hbm, bcp_ref, wout_hbm, gffn_ref,
                  wfi_hbm, wfd_hbm, gfin_ref, o_ref, kd_scr, vt_scr, z_scr, x1_scr,
                  win_ref, wap_ref, wcp_ref, wout_ref, wfi_ref, wfd_ref, stage, stage_sem):
    T = SEQ_TILE
    step = pl.program_id(0)
    first_tile = step % tiles_per_seq == 0
    slot = step % 2

    @pl.when(step == 0)
    def _():
        _load_weights([(win_hbm, win_ref), (wap_hbm, wap_ref), (wcp_hbm, wcp_ref),
                       (wout_hbm, wout_ref), (wfi_hbm, wfi_ref), (wfd_hbm, wfd_ref)],
                      stage, stage_sem)

    @pl.when(first_tile)
    def _():
        kd_scr[:, 0:WINDOW, :] = jnp.zeros((N_KV_HEADS, WINDOW, LANES), kd_scr.dtype)
        vt_scr[:, 0:WINDOW] = jnp.zeros((KV_WIDTH, WINDOW), vt_scr.dtype)
        z_scr[0:CONV_HALO, :] = jnp.zeros((CONV_HALO, CONV_CHANNELS), z_scr.dtype)

    def body(mixer, swiglu):
        acts, ffn = {}, []
        if swiglu:
            x1_prev = x1_scr[1 - slot]
            h2 = _rmsnorm(x1_prev, gffn_ref[...]).astype(jnp.bfloat16)
            ffn.append(x1_prev)

        def ffn_up(j):
            if swiglu:
                c0 = j * FFN_CHUNK
                gate = _dot(h2, wfi_ref[:, c0:c0 + FFN_CHUNK])
                up = _dot(h2, wfi_ref[:, D_FF + c0:D_FF + c0 + FFN_CHUNK])
                acts[j] = (gate * jax.nn.sigmoid(gate) * up).astype(jnp.bfloat16)

        def ffn_down(first, last):
            if swiglu:
                a = jnp.concatenate([acts.pop(j) for j in range(first, last + 1)], axis=1)
                ffn[0] = ffn[0] + _dot(a, wfd_ref[first * FFN_CHUNK:(last + 1) * FFN_CHUNK, :])

        assert N_FFN_CHUNKS == 11, "the matmul sequence below places chunks 0..10 by hand"
        if mixer:
            x = x_ref[0]
            h = _rmsnorm(x, gmix_ref[...]).astype(jnp.bfloat16)
            glu = _dot(h, win_ref[:, GLU_OFF:GATE_OFF]) + bin_ref[:, GLU_OFF:GATE_OFF]
            qkv = _dot(h, win_ref[:, 0:GLU_OFF]) + bin_ref[:, 0:GLU_OFF]
        ffn_up(0)
        if mixer:
            z_scr[CONV_HALO:CONV_HALO + T, :] = (glu[:, 0:CONV_CHANNELS]
                                                 * jax.nn.sigmoid(glu[:, CONV_CHANNELS:]))
            q = qkv[:, 0:K_OFF].astype(jnp.bfloat16)
            k = qkv[:, K_OFF:V_OFF] * (HEAD_DIM ** -0.5 * LOG2E)
            k_swapped = pltpu.roll(k, HEAD_DIM, 1)
            low_half = lax.broadcasted_iota(jnp.int32, (T, LANES), 1) < HEAD_DIM
            kd_scr[0, WINDOW:WINDOW + T, :] = jnp.where(low_half, k, k_swapped).astype(jnp.bfloat16)
            kd_scr[1, WINDOW:WINDOW + T, :] = jnp.where(low_half, k_swapped, k).astype(jnp.bfloat16)
            vt_scr[:, WINDOW:WINDOW + T] = qkv[:, V_OFF:GLU_OFF].T.astype(jnp.bfloat16)
            scores = _attn_scores(q, kd_scr)
            g_attn = jax.nn.sigmoid(_dot(h, win_ref[:, GATE_OFF:GATE_OFF + D_MODEL])
                                    + bin_ref[:, GATE_OFF:GATE_OFF + D_MODEL])
        ffn_up(1)
        if mixer:
            g_conv = jax.nn.sigmoid(_dot(h, win_ref[:, GATE_OFF + D_MODEL:IN_WIDTH])
                                    + bin_ref[:, GATE_OFF + D_MODEL:IN_WIDTH])
            probs = _attn_probs(scores, sinks_ref, first_tile)
            attn = _attn_values(probs, vt_scr, T)
            kd_scr[:, 0:WINDOW, :] = kd_scr[:, T:T + WINDOW, :]
            vt_scr[:, 0:WINDOW] = vt_scr[:, T:T + WINDOW]
        ffn_up(2)
        if mixer:
            y_attn = _dot(attn.astype(jnp.bfloat16), wap_ref[...])
        for j in range(3, N_FFN_CHUNKS):
            ffn_up(j)
        if mixer:
            acc = _causal_conv(z_scr, convw_ref, convb_ref, T)
            z_scr[0:CONV_HALO, :] = z_scr[T:T + CONV_HALO, :]
            mu = jnp.mean(acc, axis=-1, keepdims=True)
            xc = acc - mu
            var = jnp.mean(xc * xc, axis=-1, keepdims=True)
            c = xc * lax.rsqrt(var + EPS) * lng_ref[...] + lnb_ref[...]
            c = (c * jax.nn.sigmoid(c)).astype(jnp.bfloat16)
            y_conv = _dot(c, wcp_ref[...]) + bcp_ref[...]
        ffn_down(*FFN_DOWN_GROUPS[0])
        ffn_down(*FFN_DOWN_GROUPS[1])
        if mixer:
            merged = (g_attn * y_attn + g_conv * y_conv).astype(jnp.bfloat16)
        ffn_down(*FFN_DOWN_GROUPS[2])
        if mixer:
            x1_scr[slot] = x + _dot(merged, wout_ref[...])
        if swiglu:
            o_ref[0] = _rmsnorm(ffn[0], gfin_ref[...])

    pl.when(step == 0)(lambda: body(True, False))
    pl.when((step > 0) & (step < n_tiles))(lambda: body(True, True))
    pl.when(step == n_tiles)(lambda: body(False, True))


def _resident(shape):
    return pl.BlockSpec(shape, lambda i: (0,) * len(shape), pipeline_mode=pl.Buffered(1))


@jax.jit
def kernel(x, g_mix_norm, w_in, b_in, sinks, conv_w, conv_b, ln_g, ln_b, w_attn_proj,
           w_conv_proj, b_conv_proj, w_out, g_ffn_norm, w_ffn_in, w_ffn_down, g_final):
    B, S, D = x.shape
    assert D == D_MODEL and S % SEQ_TILE == 0 and SEQ_TILE % WINDOW == 0
    assert w_in.shape[0] == 1, "single-layer kernel"
    tiles_per_seq = S // SEQ_TILE
    n_tiles = B * tiles_per_seq
    row = lambda a: a.reshape(1, -1)
    weights = (w_in[0], w_attn_proj[0], w_conv_proj[0], w_out[0], w_ffn_in[0], w_ffn_down[0])
    operands = (
        sinks[0],
        x,
        row(g_mix_norm[0]), weights[0], row(b_in[0]),
        conv_w[0], row(conv_b[0]), row(ln_g[0]), row(ln_b[0]),
        weights[1], weights[2], row(b_conv_proj[0]),
        weights[3], row(g_ffn_norm[0]),
        weights[4], weights[5], row(g_final),
    )

    def tile_block(t):
        return (t // tiles_per_seq, t % tiles_per_seq, 0)

    x_spec = pl.BlockSpec((1, SEQ_TILE, D), lambda i: tile_block(jnp.minimum(i, n_tiles - 1)))
    o_spec = pl.BlockSpec((1, SEQ_TILE, D), lambda i: tile_block(jnp.maximum(i - 1, 0)))
    in_specs = [pl.BlockSpec(memory_space=pltpu.SMEM), x_spec]
    in_specs += [pl.BlockSpec(memory_space=pl.ANY) if any(a is w for w in weights) else _resident(a.shape)
                 for a in operands[2:]]
    return pl.pallas_call(
        functools.partial(_layer_kernel, tiles_per_seq, n_tiles),
        out_shape=jax.ShapeDtypeStruct(x.shape, x.dtype),
        grid=(n_tiles + 1,),
        in_specs=in_specs,
        out_specs=o_spec,
        scratch_shapes=[
            pltpu.VMEM((N_KV_HEADS, WINDOW + SEQ_TILE, LANES), jnp.bfloat16),
            pltpu.VMEM((KV_WIDTH, WINDOW + SEQ_TILE), jnp.bfloat16),
            pltpu.VMEM((SEQ_TILE + CONV_HALO, CONV_CHANNELS), jnp.float32),
            pltpu.VMEM((2, SEQ_TILE, D_MODEL), jnp.float32),
            *[pltpu.VMEM(w.shape, jnp.bfloat16) for w in weights],
            pltpu.VMEM((STAGE_SLOTS, STAGE_ROWS, STAGE_COLS), jnp.float32),
            pltpu.SemaphoreType.DMA((STAGE_SLOTS,)),
        ],
        compiler_params=pltpu.CompilerParams(
            dimension_semantics=("arbitrary",),
            vmem_limit_bytes=VMEM_LIMIT_BYTES,
        ),
        name="hybrid_layer",
    )(*operands)
```

```python
import functools

import jax
import jax.numpy as jnp
from jax import lax
from jax.experimental import pallas as pl
from jax.experimental.pallas import tpu as pltpu

D_MODEL = 1024
HEAD_DIM = 64
N_Q_HEADS = 8
N_KV_HEADS = 2
GROUP = N_Q_HEADS // N_KV_HEADS
WINDOW = 128
ATTN_WIDTH = N_Q_HEADS * HEAD_DIM
KV_WIDTH = N_KV_HEADS * HEAD_DIM
CONV_CHANNELS = 512
CONV_WIDTH = 31
CONV_HALO = 32
K_OFF = ATTN_WIDTH
V_OFF = K_OFF + KV_WIDTH
GLU_OFF = V_OFF + KV_WIDTH
GATE_OFF = GLU_OFF + 2 * CONV_CHANNELS
IN_WIDTH = GATE_OFF + 2 * D_MODEL
D_FF = 2816
EPS = 1e-5
NEG = -1e30
LOG2E = 1.4426950408889634

LANES = 128
SUBLANES = 8
SEQ_TILE = 256
FFN_CHUNK = 256
N_FFN_CHUNKS = D_FF // FFN_CHUNK
FFN_DOWN_GROUPS = ((0, 3), (4, 7), (8, 10))
CONV_ROWS = 128
STAGE_ROWS, STAGE_COLS = 512, 1024
STAGE_SLOTS = 3
VMEM_LIMIT_BYTES = 56 * 1024 * 1024


def _rmsnorm(x, g):
    return x * lax.rsqrt(jnp.mean(x * x, axis=-1, keepdims=True) + EPS) * g


def _dot(a, b):
    return jnp.dot(a, b, preferred_element_type=jnp.float32)


def _dot_nt(a, b):
    return lax.dot_general(a, b, (((1,), (1,)), ((), ())), preferred_element_type=jnp.float32)


def _load_weights(pairs, stage, sem):
    blocks = []
    for src, dst in pairs:
        rows, cols = src.shape
        assert dst.shape == src.shape
        for r0 in range(0, rows, STAGE_ROWS):
            for c0 in range(0, cols, STAGE_COLS):
                blocks.append((src, dst, r0, min(STAGE_ROWS, rows - r0), c0, min(STAGE_COLS, cols - c0)))

    def copy(i):
        src, _, r0, nr, c0, nc = blocks[i]
        slot = i % STAGE_SLOTS
        return pltpu.make_async_copy(src.at[r0:r0 + nr, c0:c0 + nc],
                                     stage.at[slot, 0:nr, 0:nc], sem.at[slot])

    for i in range(min(STAGE_SLOTS - 1, len(blocks))):
        copy(i).start()
    for i, (_, dst, r0, nr, c0, nc) in enumerate(blocks):
        if i + STAGE_SLOTS - 1 < len(blocks):
            copy(i + STAGE_SLOTS - 1).start()
        copy(i).wait()
        dst[r0:r0 + nr, c0:c0 + nc] = stage[i % STAGE_SLOTS, 0:nr, 0:nc].astype(dst.dtype)


def _attn_scores(q, kd_scr):
    T = q.shape[0]
    low_half = lax.broadcasted_iota(jnp.int32, (WINDOW, LANES), 1) < HEAD_DIM
    zero = jnp.zeros((), q.dtype)
    scores = {}
    for b in range(T // WINDOW):
        r0 = b * WINDOW
        for p in range(N_Q_HEADS // 2):
            g = (2 * p) // GROUP
            qp = q[r0:r0 + WINDOW, p * LANES:(p + 1) * LANES]
            rhs = jnp.concatenate([jnp.where(low_half, qp, zero),
                                   jnp.where(low_half, zero, qp)], axis=0)
            scores[b, p] = _dot_nt(kd_scr[g, r0:r0 + 2 * WINDOW, :], rhs)
    return scores


def _attn_probs(scores, sinks_ref, first_tile):
    kj = lax.broadcasted_iota(jnp.int32, (2 * WINDOW, 2 * LANES), 0)
    qi = lax.broadcasted_iota(jnp.int32, (2 * WINDOW, 2 * LANES), 1) % WINDOW
    diff = qi + WINDOW - kj
    band = (diff >= 0) & (diff < WINDOW)
    band_first = band & ((kj >= WINDOW) | jnp.logical_not(first_tile))
    head_lane = lax.broadcasted_iota(jnp.int32, (1, 2 * LANES), 1) < LANES
    probs = {}
    for (b, p), st in scores.items():
        st = jnp.where(band_first if b == 0 else band, st, NEG)
        sink = jnp.where(head_lane, sinks_ref[2 * p], sinks_ref[2 * p + 1]) * LOG2E
        m = jnp.maximum(jnp.max(st, axis=0, keepdims=True), sink)
        e = jnp.exp2(st - m)
        denom = jnp.sum(e, axis=0, keepdims=True) + jnp.exp2(sink - m)
        probs[b, p] = (e.astype(jnp.bfloat16), denom)
    return probs


def _attn_values(probs, vt_scr, T):
    blocks = []
    for b in range(T // WINDOW):
        r0 = b * WINDOW
        pairs = []
        for p in range(N_Q_HEADS // 2):
            g = (2 * p) // GROUP
            e, denom = probs[b, p]
            vt = vt_scr[g * HEAD_DIM:(g + 1) * HEAD_DIM, r0:r0 + 2 * WINDOW]
            ot = _dot(vt, e) / denom
            pairs.append(jnp.concatenate([ot[:, 0:LANES], ot[:, LANES:]], axis=0).T)
        blocks.append(jnp.concatenate(pairs, axis=1))
    return jnp.concatenate(blocks, axis=0)


def _causal_conv(z_scr, convw_ref, convb_ref, T):
    first = CONV_HALO - (CONV_WIDTH - 1)
    col_chunks = []
    for c in range(CONV_CHANNELS // LANES):
        cs = slice(c * LANES, (c + 1) * LANES)
        row_chunks = []
        for r0 in range(0, T, CONV_ROWS):
            acc = jnp.broadcast_to(convb_ref[:, cs], (CONV_ROWS, LANES))
            zwin = z_scr[r0:r0 + CONV_ROWS + CONV_HALO, cs]
            zb = (zwin.astype(jnp.bfloat16), zwin[SUBLANES:, :].astype(jnp.bfloat16))
            for r in range(SUBLANES):
                rows = CONV_ROWS + (SUBLANES if r else 0)
                part = None
                for o in range(first, first + CONV_WIDTH):
                    if o % SUBLANES != r:
                        continue
                    a = (o - r) // SUBLANES
                    start = (a - a % 2) * SUBLANES
                    term = (convw_ref[o - first:o - first + 1, cs].astype(jnp.bfloat16)
                            * zb[a % 2][start:start + rows, :])
                    part = term if part is None else part + term
                acc = acc + part.astype(jnp.float32)[r:r + CONV_ROWS, :]
            row_chunks.append(acc)
        col_chunks.append(jnp.concatenate(row_chunks, axis=0))
    return jnp.concatenate(col_chunks, axis=1)


def _layer_kernel(tiles_per_seq, n_tiles, sinks_ref, x_ref, gmix_ref, win_hbm, bin_ref, convw_ref,
                  convb_ref, lng_ref, lnb_ref, wap_hbm, wcp_hbm, bcp_ref, wout_hbm, gffn_ref,
                  wfi_hbm, wfd_hbm, gfin_ref, o_ref, kd_scr, vt_scr, z_scr, x1_scr,
                  win_ref, wap_ref, wcp_ref, wout_ref, wfi_ref, wfd_ref, stage, stage_sem):
    T = SEQ_TILE
    step = pl.program_id(0)
    first_tile = step % tiles_per_seq == 0

    @pl.when(step == 0)
    def _():
        _load_weights([(win_hbm, win_ref), (wap_hbm, wap_ref), (wcp_hbm, wcp_ref),
                       (wout_hbm, wout_ref), (wfi_hbm, wfi_ref), (wfd_hbm, wfd_ref)],
                      stage, stage_sem)

    @pl.when(first_tile)
    def _():
        kd_scr[:, 0:WINDOW, :] = jnp.zeros((N_KV_HEADS, WINDOW, LANES), kd_scr.dtype)
        vt_scr[:, 0:WINDOW] = jnp.zeros((KV_WIDTH, WINDOW), vt_scr.dtype)
        z_scr[0:CONV_HALO, :] = jnp.zeros((CONV_HALO, CONV_CHANNELS), z_scr.dtype)

    def body(mixer, swiglu):
        acts, ffn = {}, []
        if swiglu:
            x1_prev = x1_scr[...]
            h2 = _rmsnorm(x1_prev, gffn_ref[...]).astype(jnp.bfloat16)
            ffn.append(x1_prev)

        def ffn_up(j):
            if swiglu:
                c0 = j * FFN_CHUNK
                gate = _dot(h2, wfi_ref[:, c0:c0 + FFN_CHUNK])
                up = _dot(h2, wfi_ref[:, D_FF + c0:D_FF + c0 + FFN_CHUNK])
                acts[j] = (gate * jax.nn.sigmoid(gate) * up).astype(jnp.bfloat16)

        def ffn_down(first, last):
            if swiglu:
                a = jnp.concatenate([acts.pop(j) for j in range(first, last + 1)], axis=1)
                ffn[0] = ffn[0] + _dot(a, wfd_ref[first * FFN_CHUNK:(last + 1) * FFN_CHUNK, :])

        assert N_FFN_CHUNKS == 11, "the matmul sequence below places chunks 0..10 by hand"
        if mixer:
            x = x_ref[0]
            h = _rmsnorm(x, gmix_ref[...]).astype(jnp.bfloat16)
            glu = _dot(h, win_ref[:, GLU_OFF:GATE_OFF]) + bin_ref[:, GLU_OFF:GATE_OFF]
            qkv = _dot(h, win_ref[:, 0:GLU_OFF]) + bin_ref[:, 0:GLU_OFF]
        ffn_up(0)
        if mixer:
            z_scr[CONV_HALO:CONV_HALO + T, :] = (glu[:, 0:CONV_CHANNELS]
                                                 * jax.nn.sigmoid(glu[:, CONV_CHANNELS:]))
            q = qkv[:, 0:K_OFF].astype(jnp.bfloat16)
            k = qkv[:, K_OFF:V_OFF] * (HEAD_DIM ** -0.5 * LOG2E)
            k_swapped = pltpu.roll(k, HEAD_DIM, 1)
            low_half = lax.broadcasted_iota(jnp.int32, (T, LANES), 1) < HEAD_DIM
            kd_scr[0, WINDOW:WINDOW + T, :] = jnp.where(low_half, k, k_swapped).astype(jnp.bfloat16)
            kd_scr[1, WINDOW:WINDOW + T, :] = jnp.where(low_half, k_swapped, k).astype(jnp.bfloat16)
            vt_scr[:, WINDOW:WINDOW + T] = qkv[:, V_OFF:GLU_OFF].T.astype(jnp.bfloat16)
            scores = _attn_scores(q, kd_scr)
            g_attn = jax.nn.sigmoid(_dot(h, win_ref[:, GATE_OFF:GATE_OFF + D_MODEL])
                                    + bin_ref[:, GATE_OFF:GATE_OFF + D_MODEL])
        ffn_up(1)
        if mixer:
            g_conv = jax.nn.sigmoid(_dot(h, win_ref[:, GATE_OFF + D_MODEL:IN_WIDTH])
                                    + bin_ref[:, GATE_OFF + D_MODEL:IN_WIDTH])
            probs = _attn_probs(scores, sinks_ref, first_tile)
            attn = _attn_values(probs, vt_scr, T)
            kd_scr[:, 0:WINDOW, :] = kd_scr[:, T:T + WINDOW, :]
            vt_scr[:, 0:WINDOW] = vt_scr[:, T:T + WINDOW]
        ffn_up(2)
        if mixer:
            y_attn = _dot(attn.astype(jnp.bfloat16), wap_ref[...])
        for j in range(3, N_FFN_CHUNKS):
            ffn_up(j)
        if mixer:
            acc = _causal_conv(z_scr, convw_ref, convb_ref, T)
            z_scr[0:CONV_HALO, :] = z_scr[T:T + CONV_HALO, :]
            mu = jnp.mean(acc, axis=-1, keepdims=True)
            xc = acc - mu
            var = jnp.mean(xc * xc, axis=-1, keepdims=True)
            c = xc * lax.rsqrt(var + EPS) * lng_ref[...] + lnb_ref[...]
            c = (c * jax.nn.sigmoid(c)).astype(jnp.bfloat16)
            y_conv = _dot(c, wcp_ref[...]) + bcp_ref[...]
        ffn_down(*FFN_DOWN_GROUPS[0])
        ffn_down(*FFN_DOWN_GROUPS[1])
        if mixer:
            merged = (g_attn * y_attn + g_conv * y_conv).astype(jnp.bfloat16)
        ffn_down(*FFN_DOWN_GROUPS[2])
        if mixer:
            x1_scr[...] = x + _dot(merged, wout_ref[...])
        if swiglu:
            o_ref[0] = _rmsnorm(ffn[0], gfin_ref[...])

    pl.when(step == 0)(lambda: body(True, False))
    pl.when((step > 0) & (step < n_tiles))(lambda: body(True, True))
    pl.when(step == n_tiles)(lambda: body(False, True))


def _resident(shape):
    return pl.BlockSpec(shape, lambda i: (0,) * len(shape), pipeline_mode=pl.Buffered(1))


@jax.jit
def kernel(x, g_mix_norm, w_in, b_in, sinks, conv_w, conv_b, ln_g, ln_b, w_attn_proj,
           w_conv_proj, b_conv_proj, w_out, g_ffn_norm, w_ffn_in, w_ffn_down, g_final):
    B, S, D = x.shape
    assert D == D_MODEL and S % SEQ_TILE == 0 and SEQ_TILE % WINDOW == 0
    assert w_in.shape[0] == 1, "single-layer kernel"
    tiles_per_seq = S // SEQ_TILE
    n_tiles = B * tiles_per_seq
    row = lambda a: a.reshape(1, -1)
    weights = (w_in[0], w_attn_proj[0], w_conv_proj[0], w_out[0], w_ffn_in[0], w_ffn_down[0])
    operands = (
        sinks[0],
        x,
        row(g_mix_norm[0]), weights[0], row(b_in[0]),
        conv_w[0], row(conv_b[0]), row(ln_g[0]), row(ln_b[0]),
        weights[1], weights[2], row(b_conv_proj[0]),
        weights[3], row(g_ffn_norm[0]),
        weights[4], weights[5], row(g_final),
    )

    def tile_block(t):
        return (t // tiles_per_seq, t % tiles_per_seq, 0)

    x_spec = pl.BlockSpec((1, SEQ_TILE, D), lambda i: tile_block(jnp.minimum(i, n_tiles - 1)))
    o_spec = pl.BlockSpec((1, SEQ_TILE, D), lambda i: tile_block(jnp.maximum(i - 1, 0)))
    in_specs = [pl.BlockSpec(memory_space=pltpu.SMEM), x_spec]
    in_specs += [pl.BlockSpec(memory_space=pl.ANY) if any(a is w for w in weights) else _resident(a.shape)
                 for a in operands[2:]]
    return pl.pallas_call(
        functools.partial(_layer_kernel, tiles_per_seq, n_tiles),
        out_shape=jax.ShapeDtypeStruct(x.shape, x.dtype),
        grid=(n_tiles + 1,),
        in_specs=in_specs,
        out_specs=o_spec,
        scratch_shapes=[
            pltpu.VMEM((N_KV_HEADS, WINDOW + SEQ_TILE, LANES), jnp.bfloat16),
            pltpu.VMEM((KV_WIDTH, WINDOW + SEQ_TILE), jnp.bfloat16),
            pltpu.VMEM((SEQ_TILE + CONV_HALO, CONV_CHANNELS), jnp.float32),
            pltpu.VMEM((SEQ_TILE, D_MODEL), jnp.float32),
            *[pltpu.VMEM(w.shape, jnp.bfloat16) for w in weights],
            pltpu.VMEM((STAGE_SLOTS, STAGE_ROWS, STAGE_COLS), jnp.float32),
            pltpu.SemaphoreType.DMA((STAGE_SLOTS,)),
        ],
        compiler_params=pltpu.CompilerParams(
            dimension_semantics=("arbitrary",),
            vmem_limit_bytes=VMEM_LIMIT_BYTES,
        ),
        name="hybrid_layer",
    )(*operands)
```

```python
import functools

import jax
import jax.numpy as jnp
from jax import lax
from jax.experimental import pallas as pl
from jax.experimental.pallas import tpu as pltpu

D_MODEL = 1024
HEAD_DIM = 64
N_Q_HEADS = 8
N_KV_HEADS = 2
GROUP = N_Q_HEADS // N_KV_HEADS
WINDOW = 128
ATTN_WIDTH = N_Q_HEADS * HEAD_DIM
KV_WIDTH = N_KV_HEADS * HEAD_DIM
CONV_CHANNELS = 512
CONV_WIDTH = 31
CONV_HALO = 32
K_OFF = ATTN_WIDTH
V_OFF = K_OFF + KV_WIDTH
GLU_OFF = V_OFF + KV_WIDTH
GATE_OFF = GLU_OFF + 2 * CONV_CHANNELS
IN_WIDTH = GATE_OFF + 2 * D_MODEL
D_FF = 2816
EPS = 1e-5
NEG = -1e30
LOG2E = 1.4426950408889634

LANES = 128
SUBLANES = 8
SEQ_TILE = 256
FFN_CHUNK = 256
N_FFN_CHUNKS = D_FF // FFN_CHUNK
FFN_DOWN_GROUPS = ((0, 3), (4, 7), (8, 10))
CONV_ROWS = 128
STAGE_ROWS, STAGE_COLS = 512, 1024
STAGE_SLOTS = 3
VMEM_LIMIT_BYTES = 56 * 1024 * 1024


def _rmsnorm(x, g):
    return x * lax.rsqrt(jnp.mean(x * x, axis=-1, keepdims=True) + EPS) * g


def _dot(a, b):
    return jnp.dot(a, b, preferred_element_type=jnp.float32)


def _dot_nt(a, b):
    return lax.dot_general(a, b, (((1,), (1,)), ((), ())), preferred_element_type=jnp.float32)


def _load_weights(pairs, stage, sem):
    blocks = []
    for src, dst in pairs:
        rows, cols = src.shape
        assert dst.shape == src.shape
        for r0 in range(0, rows, STAGE_ROWS):
            for c0 in range(0, cols, STAGE_COLS):
                blocks.append((src, dst, r0, min(STAGE_ROWS, rows - r0), c0, min(STAGE_COLS, cols - c0)))

    def copy(i):
        src, _, r0, nr, c0, nc = blocks[i]
        slot = i % STAGE_SLOTS
        return pltpu.make_async_copy(src.at[r0:r0 + nr, c0:c0 + nc],
                                     stage.at[slot, 0:nr, 0:nc], sem.at[slot])

    for i in range(min(STAGE_SLOTS - 1, len(blocks))):
        copy(i).start()
    for i, (_, dst, r0, nr, c0, nc) in enumerate(blocks):
        if i + STAGE_SLOTS - 1 < len(blocks):
            copy(i + STAGE_SLOTS - 1).start()
        copy(i).wait()
        dst[r0:r0 + nr, c0:c0 + nc] = stage[i % STAGE_SLOTS, 0:nr, 0:nc].astype(dst.dtype)


def _attn_scores(q, kd_scr):
    T = q.shape[0]
    low_half = lax.broadcasted_iota(jnp.int32, (WINDOW, LANES), 1) < HEAD_DIM
    zero = jnp.zeros((), q.dtype)
    scores = {}
    for b in range(T // WINDOW):
        r0 = b * WINDOW
        for p in range(N_Q_HEADS // 2):
            g = (2 * p) // GROUP
            qp = q[r0:r0 + WINDOW, p * LANES:(p + 1) * LANES]
            rhs = jnp.concatenate([jnp.where(low_half, qp, zero),
                                   jnp.where(low_half, zero, qp)], axis=0)
            scores[b, p] = _dot_nt(kd_scr[g, r0:r0 + 2 * WINDOW, :], rhs)
    return scores


def _attn_probs(scores, sinks_ref, first_tile):
    kj = lax.broadcasted_iota(jnp.int32, (2 * WINDOW, 2 * LANES), 0)
    qi = lax.broadcasted_iota(jnp.int32, (2 * WINDOW, 2 * LANES), 1) % WINDOW
    diff = qi + WINDOW - kj
    band = (diff >= 0) & (diff < WINDOW)
    band_first = band & ((kj >= WINDOW) | jnp.logical_not(first_tile))
    head_lane = lax.broadcasted_iota(jnp.int32, (1, 2 * LANES), 1) < LANES
    probs = {}
    for (b, p), st in scores.items():
        st = jnp.where(band_first if b == 0 else band, st, NEG * LOG2E)
        sink = jnp.where(head_lane, sinks_ref[2 * p], sinks_ref[2 * p + 1]) * LOG2E
        m = jnp.maximum(jnp.max(st, axis=0, keepdims=True), sink)
        e = jnp.exp2(st - m)
        denom = jnp.sum(e, axis=0, keepdims=True) + jnp.exp2(sink - m)
        probs[b, p] = (e.astype(jnp.bfloat16), denom)
    return probs


def _attn_values(probs, vt_scr, T):
    blocks = []
    for b in range(T // WINDOW):
        r0 = b * WINDOW
        pairs = []
        for p in range(N_Q_HEADS // 2):
            g = (2 * p) // GROUP
            e, denom = probs[b, p]
            vt = vt_scr[g * HEAD_DIM:(g + 1) * HEAD_DIM, r0:r0 + 2 * WINDOW]
            ot = _dot(vt, e) / denom
            pairs.append(jnp.concatenate([ot[:, 0:LANES], ot[:, LANES:]], axis=0).T)
        blocks.append(jnp.concatenate(pairs, axis=1))
    return jnp.concatenate(blocks, axis=0)


def _causal_conv(z_scr, convw_ref, convb_ref, T):
    first = CONV_HALO - (CONV_WIDTH - 1)
    col_chunks = []
    for c in range(CONV_CHANNELS // LANES):
        cs = slice(c * LANES, (c + 1) * LANES)
        row_chunks = []
        for r0 in range(0, T, CONV_ROWS):
            acc = jnp.broadcast_to(convb_ref[:, cs], (CONV_ROWS, LANES))
            zwin = z_scr[r0:r0 + CONV_ROWS + CONV_HALO, cs]
            zb = (zwin.astype(jnp.bfloat16), zwin[SUBLANES:, :].astype(jnp.bfloat16))
            for r in range(SUBLANES):
                rows = CONV_ROWS + (SUBLANES if r else 0)
                part = None
                for o in range(first, first + CONV_WIDTH):
                    if o % SUBLANES != r:
                        continue
                    a = (o - r) // SUBLANES
                    start = (a - a % 2) * SUBLANES
                    term = (convw_ref[o - first:o - first + 1, cs].astype(jnp.bfloat16)
                            * zb[a % 2][start:start + rows, :])
                    part = term if part is None else part + term
                acc = acc + part.astype(jnp.float32)[r:r + CONV_ROWS, :]
            row_chunks.append(acc)
        col_chunks.append(jnp.concatenate(row_chunks, axis=0))
    return jnp.concatenate(col_chunks, axis=1)


def _layer_kernel(tiles_per_seq, n_tiles, sinks_ref, x_ref, gmix_ref, win_hbm, bin_ref, convw_ref,
                  convb_ref, lng_ref, lnb_ref, wap_hbm, wcp_hbm, bcp_ref, wout_hbm, gffn_ref,
                  wfi_hbm, wfd_hbm, gfin_ref, o_ref, kd_scr, vt_scr, z_scr, x1_scr,
                  win_ref, wap_ref, wcp_ref, wout_ref, wfi_ref, wfd_ref, stage, stage_sem):
    T = SEQ_TILE
    step = pl.program_id(0)
    first_tile = step % tiles_per_seq == 0

    @pl.when(step == 0)
    def _():
        _load_weights([(win_hbm, win_ref), (wap_hbm, wap_ref), (wcp_hbm, wcp_ref),
                       (wout_hbm, wout_ref), (wfi_hbm, wfi_ref), (wfd_hbm, wfd_ref)],
                      stage, stage_sem)

    @pl.when(first_tile)
    def _():
        kd_scr[:, 0:WINDOW, :] = jnp.zeros((N_KV_HEADS, WINDOW, LANES), kd_scr.dtype)
        vt_scr[:, 0:WINDOW] = jnp.zeros((KV_WIDTH, WINDOW), vt_scr.dtype)
        z_scr[0:CONV_HALO, :] = jnp.zeros((CONV_HALO, CONV_CHANNELS), z_scr.dtype)

    def body(mixer, swiglu):
        acts, ffn = {}, []
        if swiglu:
            x1_prev = x1_scr[...]
            h2 = _rmsnorm(x1_prev, gffn_ref[...]).astype(jnp.bfloat16)
            ffn.append(x1_prev)

        def ffn_up(j):
            if swiglu:
                c0 = j * FFN_CHUNK
                gate = _dot(h2, wfi_ref[:, c0:c0 + FFN_CHUNK])
                up = _dot(h2, wfi_ref[:, D_FF + c0:D_FF + c0 + FFN_CHUNK])
                acts[j] = (gate * jax.nn.sigmoid(gate) * up).astype(jnp.bfloat16)

        def ffn_down(first, last):
            if swiglu:
                a = jnp.concatenate([acts.pop(j) for j in range(first, last + 1)], axis=1)
                ffn[0] = ffn[0] + _dot(a, wfd_ref[first * FFN_CHUNK:(last + 1) * FFN_CHUNK, :])

        assert N_FFN_CHUNKS == 11, "the matmul sequence below places chunks 0..10 by hand"
        if mixer:
            x = x_ref[0]
            h = _rmsnorm(x, gmix_ref[...]).astype(jnp.bfloat16)
            glu = _dot(h, win_ref[:, GLU_OFF:GATE_OFF]) + bin_ref[:, GLU_OFF:GATE_OFF]
            qkv = _dot(h, win_ref[:, 0:GLU_OFF]) + bin_ref[:, 0:GLU_OFF]
        ffn_up(0)
        if mixer:
            z_scr[CONV_HALO:CONV_HALO + T, :] = (glu[:, 0:CONV_CHANNELS]
                                                 * jax.nn.sigmoid(glu[:, CONV_CHANNELS:]))
            q = qkv[:, 0:K_OFF].astype(jnp.bfloat16)
            k = qkv[:, K_OFF:V_OFF] * (HEAD_DIM ** -0.5 * LOG2E)
            k_swapped = pltpu.roll(k, HEAD_DIM, 1)
            low_half = lax.broadcasted_iota(jnp.int32, (T, LANES), 1) < HEAD_DIM
            kd_scr[0, WINDOW:WINDOW + T, :] = jnp.where(low_half, k, k_swapped).astype(jnp.bfloat16)
            kd_scr[1, WINDOW:WINDOW + T, :] = jnp.where(low_half, k_swapped, k).astype(jnp.bfloat16)
            vt_scr[:, WINDOW:WINDOW + T] = qkv[:, V_OFF:GLU_OFF].T.astype(jnp.bfloat16)
            scores = _attn_scores(q, kd_scr)
            g_attn = jax.nn.sigmoid(_dot(h, win_ref[:, GATE_OFF:GATE_OFF + D_MODEL])
                                    + bin_ref[:, GATE_OFF:GATE_OFF + D_MODEL])
        ffn_up(1)
        if mixer:
            g_conv = jax.nn.sigmoid(_dot(h, win_ref[:, GATE_OFF + D_MODEL:IN_WIDTH])
                                    + bin_ref[:, GATE_OFF + D_MODEL:IN_WIDTH])
            probs = _attn_probs(scores, sinks_ref, first_tile)
            attn = _attn_values(probs, vt_scr, T)
            kd_scr[:, 0:WINDOW, :] = kd_scr[:, T:T + WINDOW, :]
            vt_scr[:, 0:WINDOW] = vt_scr[:, T:T + WINDOW]
        ffn_up(2)
        if mixer:
            y_attn = _dot(attn.astype(jnp.bfloat16), wap_ref[...])
        for j in range(3, N_FFN_CHUNKS):
            ffn_up(j)
        if mixer:
            acc = _causal_conv(z_scr, convw_ref, convb_ref, T)
            z_scr[0:CONV_HALO, :] = z_scr[T:T + CONV_HALO, :]
            mu = jnp.mean(acc, axis=-1, keepdims=True)
            xc = acc - mu
            var = jnp.mean(xc * xc, axis=-1, keepdims=True)
            c = xc * lax.rsqrt(var + EPS) * lng_ref[...] + lnb_ref[...]
            c = (c * jax.nn.sigmoid(c)).astype(jnp.bfloat16)
            y_conv = _dot(c, wcp_ref[...]) + bcp_ref[...]
        ffn_down(*FFN_DOWN_GROUPS[0])
        ffn_down(*FFN_DOWN_GROUPS[1])
        if mixer:
            merged = (g_attn * y_attn + g_conv * y_conv).astype(jnp.bfloat16)
        ffn_down(*FFN_DOWN_GROUPS[2])
        if mixer:
            x1_scr[...] = x + _dot(merged, wout_ref[...])
        if swiglu:
            o_ref[0] = _rmsnorm(ffn[0], gfin_ref[...])

    pl.when(step == 0)(lambda: body(True, False))
    pl.when((step > 0) & (step < n_tiles))(lambda: body(True, True))
    pl.when(step == n_tiles)(lambda: body(False, True))


def _resident(shape):
    return pl.BlockSpec(shape, lambda i: (0,) * len(shape), pipeline_mode=pl.Buffered(1))


@jax.jit
def kernel(x, g_mix_norm, w_in, b_in, sinks, conv_w, conv_b, ln_g, ln_b, w_attn_proj,
           w_conv_proj, b_conv_proj, w_out, g_ffn_norm, w_ffn_in, w_ffn_down, g_final):
    B, S, D = x.shape
    assert D == D_MODEL and S % SEQ_TILE == 0 and SEQ_TILE % WINDOW == 0
    assert w_in.shape[0] == 1, "single-layer kernel"
    tiles_per_seq = S // SEQ_TILE
    n_tiles = B * tiles_per_seq
    row = lambda a: a.reshape(1, -1)
    weights = (w_in[0], w_attn_proj[0], w_conv_proj[0], w_out[0], w_ffn_in[0], w_ffn_down[0])
    operands = (
        sinks[0],
        x,
        row(g_mix_norm[0]), weights[0], row(b_in[0]),
        conv_w[0], row(conv_b[0]), row(ln_g[0]), row(ln_b[0]),
        weights[1], weights[2], row(b_conv_proj[0]),
        weights[3], row(g_ffn_norm[0]),
        weights[4], weights[5], row(g_final),
    )

    def tile_block(t):
        return (t // tiles_per_seq, t % tiles_per_seq, 0)

    x_spec = pl.BlockSpec((1, SEQ_TILE, D), lambda i: tile_block(jnp.minimum(i, n_tiles - 1)))
    o_spec = pl.BlockSpec((1, SEQ_TILE, D), lambda i: tile_block(jnp.maximum(i - 1, 0)))
    in_specs = [pl.BlockSpec(memory_space=pltpu.SMEM), x_spec]
    in_specs += [pl.BlockSpec(memory_space=pl.ANY) if any(a is w for w in weights) else _resident(a.shape)
                 for a in operands[2:]]
    return pl.pallas_call(
        functools.partial(_layer_kernel, tiles_per_seq, n_tiles),
        out_shape=jax.ShapeDtypeStruct(x.shape, x.dtype),
        grid=(n_tiles + 1,),
        in_specs=in_specs,
        out_specs=o_spec,
        scratch_shapes=[
            pltpu.VMEM((N_KV_HEADS, WINDOW + SEQ_TILE, LANES), jnp.bfloat16),
            pltpu.VMEM((KV_WIDTH, WINDOW + SEQ_TILE), jnp.bfloat16),
            pltpu.VMEM((SEQ_TILE + CONV_HALO, CONV_CHANNELS), jnp.float32),
            pltpu.VMEM((SEQ_TILE, D_MODEL), jnp.float32),
            *[pltpu.VMEM(w.shape, jnp.bfloat16) for w in weights],
            pltpu.VMEM((STAGE_SLOTS, STAGE_ROWS, STAGE_COLS), jnp.float32),
            pltpu.SemaphoreType.DMA((STAGE_SLOTS,)),
        ],
        compiler_params=pltpu.CompilerParams(
            dimension_semantics=("arbitrary",),
            vmem_limit_bytes=VMEM_LIMIT_BYTES,
        ),
        name="hybrid_layer",
    )(*operands)
```

```python
import functools

import jax
import jax.numpy as jnp
from jax import lax
from jax.experimental import pallas as pl
from jax.experimental.pallas import tpu as pltpu

D_MODEL = 1024
HEAD_DIM = 64
N_Q_HEADS = 8
N_KV_HEADS = 2
GROUP = N_Q_HEADS // N_KV_HEADS
WINDOW = 128
ATTN_WIDTH = N_Q_HEADS * HEAD_DIM
KV_WIDTH = N_KV_HEADS * HEAD_DIM
CONV_CHANNELS = 512
CONV_WIDTH = 31
CONV_HALO = 32
K_OFF = ATTN_WIDTH
V_OFF = K_OFF + KV_WIDTH
GLU_OFF = V_OFF + KV_WIDTH
GATE_OFF = GLU_OFF + 2 * CONV_CHANNELS
IN_WIDTH = GATE_OFF + 2 * D_MODEL
D_FF = 2816
EPS = 1e-5
NEG = -1e30
LOG2E = 1.4426950408889634

LANES = 128
SUBLANES = 8
SEQ_TILE = 256
FFN_CHUNK = 256
N_FFN_CHUNKS = D_FF // FFN_CHUNK
FFN_DOWN_GROUPS = ((0, 3), (4, 7), (8, 10))
CONV_ROWS = 128
STAGE_ROWS, STAGE_COLS = 512, 1024
STAGE_SLOTS = 3
VMEM_LIMIT_BYTES = 56 * 1024 * 1024


def _rmsnorm(x, g):
    return x * lax.rsqrt(jnp.mean(x * x, axis=-1, keepdims=True) + EPS) * g


def _dot(a, b):
    return jnp.dot(a, b, preferred_element_type=jnp.float32)


def _dot_nt(a, b):
    return lax.dot_general(a, b, (((1,), (1,)), ((), ())), preferred_element_type=jnp.float32)


def _load_weights(pairs, stage, sem):
    blocks = []
    for src, dst in pairs:
        rows, cols = src.shape
        assert dst.shape == src.shape
        for r0 in range(0, rows, STAGE_ROWS):
            for c0 in range(0, cols, STAGE_COLS):
                blocks.append((src, dst, r0, min(STAGE_ROWS, rows - r0), c0, min(STAGE_COLS, cols - c0)))

    def copy(i):
        src, _, r0, nr, c0, nc = blocks[i]
        slot = i % STAGE_SLOTS
        return pltpu.make_async_copy(src.at[r0:r0 + nr, c0:c0 + nc],
                                     stage.at[slot, 0:nr, 0:nc], sem.at[slot])

    for i in range(min(STAGE_SLOTS - 1, len(blocks))):
        copy(i).start()
    for i, (_, dst, r0, nr, c0, nc) in enumerate(blocks):
        if i + STAGE_SLOTS - 1 < len(blocks):
            copy(i + STAGE_SLOTS - 1).start()
        copy(i).wait()
        dst[r0:r0 + nr, c0:c0 + nc] = stage[i % STAGE_SLOTS, 0:nr, 0:nc].astype(dst.dtype)


def _attn_scores(q, kd_scr):
    T = q.shape[0]
    low_half = lax.broadcasted_iota(jnp.int32, (WINDOW, LANES), 1) < HEAD_DIM
    zero = jnp.zeros((), q.dtype)
    scores = {}
    for b in range(T // WINDOW):
        r0 = b * WINDOW
        for p in range(N_Q_HEADS // 2):
            g = (2 * p) // GROUP
            qp = q[r0:r0 + WINDOW, p * LANES:(p + 1) * LANES]
            rhs = jnp.concatenate([jnp.where(low_half, qp, zero),
                                   jnp.where(low_half, zero, qp)], axis=0)
            scores[b, p] = _dot_nt(kd_scr[g, r0:r0 + 2 * WINDOW, :], rhs)
    return scores


def _attn_probs(scores, sinks_ref, first_tile):
    kj = lax.broadcasted_iota(jnp.int32, (2 * WINDOW, 2 * LANES), 0)
    qi = lax.broadcasted_iota(jnp.int32, (2 * WINDOW, 2 * LANES), 1) % WINDOW
    diff = qi + WINDOW - kj
    band = (diff >= 0) & (diff < WINDOW)
    band_first = band & ((kj >= WINDOW) | jnp.logical_not(first_tile))
    head_lane = lax.broadcasted_iota(jnp.int32, (1, 2 * LANES), 1) < LANES
    probs = {}
    for (b, p), st in scores.items():
        st = jnp.where(band_first if b == 0 else band, st, NEG * LOG2E)
        sink = jnp.where(head_lane, sinks_ref[2 * p], sinks_ref[2 * p + 1]) * LOG2E
        m = jnp.maximum(jnp.max(st, axis=0, keepdims=True), sink)
        e = jnp.exp2(st - m)
        denom = jnp.sum(e, axis=0, keepdims=True) + jnp.exp2(sink - m)
        probs[b, p] = (e.astype(jnp.bfloat16), denom)
    return probs


def _attn_values(probs, vt_scr, T):
    blocks = []
    for b in range(T // WINDOW):
        r0 = b * WINDOW
        pairs = []
        for p in range(N_Q_HEADS // 2):
            g = (2 * p) // GROUP
            e, denom = probs[b, p]
            vt = vt_scr[g * HEAD_DIM:(g + 1) * HEAD_DIM, r0:r0 + 2 * WINDOW]
            ot = _dot(vt, e) / denom
            pairs.append(jnp.concatenate([ot[:, 0:LANES], ot[:, LANES:]], axis=0).T)
        blocks.append(jnp.concatenate(pairs, axis=1))
    return jnp.concatenate(blocks, axis=0)


def _causal_conv(z_scr, convw_ref, convb_ref, T):
    first = CONV_HALO - (CONV_WIDTH - 1)
    col_chunks = []
    for c in range(CONV_CHANNELS // LANES):
        cs = slice(c * LANES, (c + 1) * LANES)
        row_chunks = []
        for r0 in range(0, T, CONV_ROWS):
            acc = jnp.broadcast_to(convb_ref[:, cs], (CONV_ROWS, LANES))
            zwin = z_scr[r0:r0 + CONV_ROWS + CONV_HALO, cs]
            for r in range(SUBLANES):
                rows = CONV_ROWS + (SUBLANES if r else 0)
                part = None
                for o in range(first, first + CONV_WIDTH):
                    if o % SUBLANES != r:
                        continue
                    term = convw_ref[o - first:o - first + 1, cs] * zwin[o - r:o - r + rows, :]
                    part = term if part is None else part + term
                acc = acc + part[r:r + CONV_ROWS, :]
            row_chunks.append(acc)
        col_chunks.append(jnp.concatenate(row_chunks, axis=0))
    return jnp.concatenate(col_chunks, axis=1)


def _layer_kernel(tiles_per_seq, n_tiles, sinks_ref, x_ref, gmix_ref, win_hbm, bin_ref, convw_ref,
                  convb_ref, lng_ref, lnb_ref, wap_hbm, wcp_hbm, bcp_ref, wout_hbm, gffn_ref,
                  wfi_hbm, wfd_hbm, gfin_ref, o_ref, kd_scr, vt_scr, z_scr, x1_scr,
                  win_ref, wap_ref, wcp_ref, wout_ref, wfi_ref, wfd_ref, stage, stage_sem):
    T = SEQ_TILE
    step = pl.program_id(0)
    first_tile = step % tiles_per_seq == 0

    @pl.when(step == 0)
    def _():
        _load_weights([(win_hbm, win_ref), (wap_hbm, wap_ref), (wcp_hbm, wcp_ref),
                       (wout_hbm, wout_ref), (wfi_hbm, wfi_ref), (wfd_hbm, wfd_ref)],
                      stage, stage_sem)

    @pl.when(first_tile)
    def _():
        kd_scr[:, 0:WINDOW, :] = jnp.zeros((N_KV_HEADS, WINDOW, LANES), kd_scr.dtype)
        vt_scr[:, 0:WINDOW] = jnp.zeros((KV_WIDTH, WINDOW), vt_scr.dtype)
        z_scr[0:CONV_HALO, :] = jnp.zeros((CONV_HALO, CONV_CHANNELS), z_scr.dtype)

    def body(mixer, swiglu):
        acts, ffn = {}, []
        if swiglu:
            x1_prev = x1_scr[...]
            h2 = _rmsnorm(x1_prev, gffn_ref[...]).astype(jnp.bfloat16)
            ffn.append(x1_prev)

        def ffn_up(j):
            if swiglu:
                c0 = j * FFN_CHUNK
                gate = _dot(h2, wfi_ref[:, c0:c0 + FFN_CHUNK])
                up = _dot(h2, wfi_ref[:, D_FF + c0:D_FF + c0 + FFN_CHUNK])
                acts[j] = (gate * jax.nn.sigmoid(gate) * up).astype(jnp.bfloat16)

        def ffn_down(first, last):
            if swiglu:
                a = jnp.concatenate([acts.pop(j) for j in range(first, last + 1)], axis=1)
                ffn[0] = ffn[0] + _dot(a, wfd_ref[first * FFN_CHUNK:(last + 1) * FFN_CHUNK, :])

        assert N_FFN_CHUNKS == 11, "the matmul sequence below places chunks 0..10 by hand"
        if mixer:
            x = x_ref[0]
            h = _rmsnorm(x, gmix_ref[...]).astype(jnp.bfloat16)
            glu = _dot(h, win_ref[:, GLU_OFF:GATE_OFF]) + bin_ref[:, GLU_OFF:GATE_OFF]
            qkv = _dot(h, win_ref[:, 0:GLU_OFF]) + bin_ref[:, 0:GLU_OFF]
        ffn_up(0)
        if mixer:
            z_scr[CONV_HALO:CONV_HALO + T, :] = (glu[:, 0:CONV_CHANNELS]
                                                 * jax.nn.sigmoid(glu[:, CONV_CHANNELS:]))
            q = qkv[:, 0:K_OFF].astype(jnp.bfloat16)
            k = qkv[:, K_OFF:V_OFF] * (HEAD_DIM ** -0.5 * LOG2E)
            k_swapped = pltpu.roll(k, HEAD_DIM, 1)
            low_half = lax.broadcasted_iota(jnp.int32, (T, LANES), 1) < HEAD_DIM
            kd_scr[0, WINDOW:WINDOW + T, :] = jnp.where(low_half, k, k_swapped).astype(jnp.bfloat16)
            kd_scr[1, WINDOW:WINDOW + T, :] = jnp.where(low_half, k_swapped, k).astype(jnp.bfloat16)
            vt_scr[:, WINDOW:WINDOW + T] = qkv[:, V_OFF:GLU_OFF].T.astype(jnp.bfloat16)
            scores = _attn_scores(q, kd_scr)
            g_attn = jax.nn.sigmoid(_dot(h, win_ref[:, GATE_OFF:GATE_OFF + D_MODEL])
                                    + bin_ref[:, GATE_OFF:GATE_OFF + D_MODEL])
        ffn_up(1)
        if mixer:
            g_conv = jax.nn.sigmoid(_dot(h, win_ref[:, GATE_OFF + D_MODEL:IN_WIDTH])
                                    + bin_ref[:, GATE_OFF + D_MODEL:IN_WIDTH])
            probs = _attn_probs(scores, sinks_ref, first_tile)
            attn = _attn_values(probs, vt_scr, T)
            kd_scr[:, 0:WINDOW, :] = kd_scr[:, T:T + WINDOW, :]
            vt_scr[:, 0:WINDOW] = vt_scr[:, T:T + WINDOW]
        ffn_up(2)
        if mixer:
            y_attn = _dot(attn.astype(jnp.bfloat16), wap_ref[...])
        for j in range(3, N_FFN_CHUNKS):
            ffn_up(j)
        if mixer:
            acc = _causal_conv(z_scr, convw_ref, convb_ref, T)
            z_scr[0:CONV_HALO, :] = z_scr[T:T + CONV_HALO, :]
            mu = jnp.mean(acc, axis=-1, keepdims=True)
            xc = acc - mu
            var = jnp.mean(xc * xc, axis=-1, keepdims=True)
            c = xc * lax.rsqrt(var + EPS) * lng_ref[...] + lnb_ref[...]
            c = (c * jax.nn.sigmoid(c)).astype(jnp.bfloat16)
            y_conv = _dot(c, wcp_ref[...]) + bcp_ref[...]
        ffn_down(*FFN_DOWN_GROUPS[0])
        ffn_down(*FFN_DOWN_GROUPS[1])
        if mixer:
            merged = (g_attn * y_attn + g_conv * y_conv).astype(jnp.bfloat16)
        ffn_down(*FFN_DOWN_GROUPS[2])
        if mixer:
            x1_scr[...] = x + _dot(merged, wout_ref[...])
        if swiglu:
            o_ref[0] = _rmsnorm(ffn[0], gfin_ref[...])

    pl.when(step == 0)(lambda: body(True, False))
    pl.when((step > 0) & (step < n_tiles))(lambda: body(True, True))
    pl.when(step == n_tiles)(lambda: body(False, True))


def _resident(shape):
    return pl.BlockSpec(shape, lambda i: (0,) * len(shape), pipeline_mode=pl.Buffered(1))


@jax.jit
def kernel(x, g_mix_norm, w_in, b_in, sinks, conv_w, conv_b, ln_g, ln_b, w_attn_proj,
           w_conv_proj, b_conv_proj, w_out, g_ffn_norm, w_ffn_in, w_ffn_down, g_final):
    B, S, D = x.shape
    assert D == D_MODEL and S % SEQ_TILE == 0 and SEQ_TILE % WINDOW == 0
    assert w_in.shape[0] == 1, "single-layer kernel"
    tiles_per_seq = S // SEQ_TILE
    n_tiles = B * tiles_per_seq
    row = lambda a: a.reshape(1, -1)
    weights = (w_in[0], w_attn_proj[0], w_conv_proj[0], w_out[0], w_ffn_in[0], w_ffn_down[0])
    operands = (
        sinks[0],
        x,
        row(g_mix_norm[0]), weights[0], row(b_in[0]),
        conv_w[0], row(conv_b[0]), row(ln_g[0]), row(ln_b[0]),
        weights[1], weights[2], row(b_conv_proj[0]),
        weights[3], row(g_ffn_norm[0]),
        weights[4], weights[5], row(g_final),
    )

    def tile_block(t):
        return (t // tiles_per_seq, t % tiles_per_seq, 0)

    x_spec = pl.BlockSpec((1, SEQ_TILE, D), lambda i: tile_block(jnp.minimum(i, n_tiles - 1)))
    o_spec = pl.BlockSpec((1, SEQ_TILE, D), lambda i: tile_block(jnp.maximum(i - 1, 0)))
    in_specs = [pl.BlockSpec(memory_space=pltpu.SMEM), x_spec]
    in_specs += [pl.BlockSpec(memory_space=pl.ANY) if any(a is w for w in weights) else _resident(a.shape)
                 for a in operands[2:]]
    return pl.pallas_call(
        functools.partial(_layer_kernel, tiles_per_seq, n_tiles),
        out_shape=jax.ShapeDtypeStruct(x.shape, x.dtype),
        grid=(n_tiles + 1,),
        in_specs=in_specs,
        out_specs=o_spec,
        scratch_shapes=[
            pltpu.VMEM((N_KV_HEADS, WINDOW + SEQ_TILE, LANES), jnp.bfloat16),
            pltpu.VMEM((KV_WIDTH, WINDOW + SEQ_TILE), jnp.bfloat16),
            pltpu.VMEM((SEQ_TILE + CONV_HALO, CONV_CHANNELS), jnp.float32),
            pltpu.VMEM((SEQ_TILE, D_MODEL), jnp.float32),
            *[pltpu.VMEM(w.shape, jnp.bfloat16) for w in weights],
            pltpu.VMEM((STAGE_SLOTS, STAGE_ROWS, STAGE_COLS), jnp.float32),
            pltpu.SemaphoreType.DMA((STAGE_SLOTS,)),
        ],
        compiler_params=pltpu.CompilerParams(
            dimension_semantics=("arbitrary",),
            vmem_limit_bytes=VMEM_LIMIT_BYTES,
        ),
        name="hybrid_layer",
    )(*operands)
```

```python
import functools

import jax
import jax.numpy as jnp
from jax import lax
from jax.experimental import pallas as pl
from jax.experimental.pallas import tpu as pltpu

D_MODEL = 1024
HEAD_DIM = 64
N_Q_HEADS = 8
N_KV_HEADS = 2
GROUP = N_Q_HEADS // N_KV_HEADS
WINDOW = 128
ATTN_WIDTH = N_Q_HEADS * HEAD_DIM
KV_WIDTH = N_KV_HEADS * HEAD_DIM
CONV_CHANNELS = 512
CONV_WIDTH = 31
CONV_HALO = 32
K_OFF = ATTN_WIDTH
V_OFF = K_OFF + KV_WIDTH
GLU_OFF = V_OFF + KV_WIDTH
GATE_OFF = GLU_OFF + 2 * CONV_CHANNELS
IN_WIDTH = GATE_OFF + 2 * D_MODEL
D_FF = 2816
EPS = 1e-5
NEG = -1e30
LOG2E = 1.4426950408889634

LANES = 128
SUBLANES = 8
SEQ_TILE = 256
FFN_CHUNK = 256
N_FFN_CHUNKS = D_FF // FFN_CHUNK
FFN_DOWN_GROUPS = ((0, 3), (4, 7), (8, 10))
CONV_ROWS = 128
STAGE_ROWS, STAGE_COLS = 512, 1024
STAGE_SLOTS = 3
VMEM_LIMIT_BYTES = 56 * 1024 * 1024


def _rmsnorm(x, g):
    return x * lax.rsqrt(jnp.mean(x * x, axis=-1, keepdims=True) + EPS) * g


def _dot(a, b):
    return jnp.dot(a, b, preferred_element_type=jnp.float32)


def _dot_nt(a, b):
    return lax.dot_general(a, b, (((1,), (1,)), ((), ())), preferred_element_type=jnp.float32)


def _silu(x):
    h = 0.5 * x
    return h + h * jnp.tanh(h)


def _load_weights(pairs, stage, sem):
    blocks = []
    for src, dst in pairs:
        rows, cols = src.shape
        assert dst.shape == src.shape
        for r0 in range(0, rows, STAGE_ROWS):
            for c0 in range(0, cols, STAGE_COLS):
                blocks.append((src, dst, r0, min(STAGE_ROWS, rows - r0), c0, min(STAGE_COLS, cols - c0)))

    def copy(i):
        src, _, r0, nr, c0, nc = blocks[i]
        slot = i % STAGE_SLOTS
        return pltpu.make_async_copy(src.at[r0:r0 + nr, c0:c0 + nc],
                                     stage.at[slot, 0:nr, 0:nc], sem.at[slot])

    for i in range(min(STAGE_SLOTS - 1, len(blocks))):
        copy(i).start()
    for i, (_, dst, r0, nr, c0, nc) in enumerate(blocks):
        if i + STAGE_SLOTS - 1 < len(blocks):
            copy(i + STAGE_SLOTS - 1).start()
        copy(i).wait()
        dst[r0:r0 + nr, c0:c0 + nc] = stage[i % STAGE_SLOTS, 0:nr, 0:nc].astype(dst.dtype)


def _attn_scores(q, kd_scr):
    T = q.shape[0]
    low_half = lax.broadcasted_iota(jnp.int32, (WINDOW, LANES), 1) < HEAD_DIM
    zero = jnp.zeros((), q.dtype)
    scores = {}
    for b in range(T // WINDOW):
        r0 = b * WINDOW
        for p in range(N_Q_HEADS // 2):
            g = (2 * p) // GROUP
            qp = q[r0:r0 + WINDOW, p * LANES:(p + 1) * LANES]
            rhs = jnp.concatenate([jnp.where(low_half, qp, zero),
                                   jnp.where(low_half, zero, qp)], axis=0)
            scores[b, p] = _dot_nt(kd_scr[g, r0:r0 + 2 * WINDOW, :], rhs)
    return scores


def _attn_probs(scores, sinks_ref, first_tile):
    kj = lax.broadcasted_iota(jnp.int32, (2 * WINDOW, 2 * LANES), 0)
    qi = lax.broadcasted_iota(jnp.int32, (2 * WINDOW, 2 * LANES), 1) % WINDOW
    diff = qi + WINDOW - kj
    band = (diff >= 0) & (diff < WINDOW)
    band_first = band & ((kj >= WINDOW) | jnp.logical_not(first_tile))
    head_lane = lax.broadcasted_iota(jnp.int32, (1, 2 * LANES), 1) < LANES
    probs = {}
    for (b, p), st in scores.items():
        st = jnp.where(band_first if b == 0 else band, st, NEG * LOG2E)
        sink = jnp.where(head_lane, sinks_ref[2 * p], sinks_ref[2 * p + 1]) * LOG2E
        m = jnp.maximum(jnp.max(st, axis=0, keepdims=True), sink)
        e = jnp.exp2(st - m)
        denom = jnp.sum(e, axis=0, keepdims=True) + jnp.exp2(sink - m)
        probs[b, p] = (e.astype(jnp.bfloat16), denom)
    return probs


def _attn_values(probs, vt_scr, T):
    blocks = []
    for b in range(T // WINDOW):
        r0 = b * WINDOW
        pairs = []
        for p in range(N_Q_HEADS // 2):
            g = (2 * p) // GROUP
            e, denom = probs[b, p]
            vt = vt_scr[g * HEAD_DIM:(g + 1) * HEAD_DIM, r0:r0 + 2 * WINDOW]
            ot = _dot(vt, e) / denom
            pairs.append(jnp.concatenate([ot[:, 0:LANES], ot[:, LANES:]], axis=0).T)
        blocks.append(jnp.concatenate(pairs, axis=1))
    return jnp.concatenate(blocks, axis=0)


def _causal_conv(z_scr, convw_ref, convb_ref, T):
    first = CONV_HALO - (CONV_WIDTH - 1)
    col_chunks = []
    for c in range(CONV_CHANNELS // LANES):
        cs = slice(c * LANES, (c + 1) * LANES)
        row_chunks = []
        for r0 in range(0, T, CONV_ROWS):
            acc = jnp.broadcast_to(convb_ref[:, cs], (CONV_ROWS, LANES))
            zwin = z_scr[r0:r0 + CONV_ROWS + CONV_HALO, cs]
            zb = (zwin.astype(jnp.bfloat16), zwin[SUBLANES:, :].astype(jnp.bfloat16))
            for r in range(SUBLANES):
                rows = CONV_ROWS + (SUBLANES if r else 0)
                part = None
                for o in range(first, first + CONV_WIDTH):
                    if o % SUBLANES != r:
                        continue
                    a = (o - r) // SUBLANES
                    start = (a - a % 2) * SUBLANES
                    term = (convw_ref[o - first:o - first + 1, cs].astype(jnp.bfloat16)
                            * zb[a % 2][start:start + rows, :])
                    part = term if part is None else part + term
                acc = acc + part.astype(jnp.float32)[r:r + CONV_ROWS, :]
            row_chunks.append(acc)
        col_chunks.append(jnp.concatenate(row_chunks, axis=0))
    return jnp.concatenate(col_chunks, axis=1)


def _layer_kernel(tiles_per_seq, n_tiles, sinks_ref, x_ref, gmix_ref, win_hbm, bin_ref, convw_ref,
                  convb_ref, lng_ref, lnb_ref, wap_hbm, wcp_hbm, bcp_ref, wout_hbm, gffn_ref,
                  wfi_hbm, wfd_hbm, gfin_ref, o_ref, kd_scr, vt_scr, z_scr, x1_scr,
                  win_ref, wap_ref, wcp_ref, wout_ref, wfi_ref, wfd_ref, stage, stage_sem):
    T = SEQ_TILE
    step = pl.program_id(0)
    first_tile = step % tiles_per_seq == 0

    @pl.when(step == 0)
    def _():
        _load_weights([(win_hbm, win_ref), (wap_hbm, wap_ref), (wcp_hbm, wcp_ref),
                       (wout_hbm, wout_ref), (wfi_hbm, wfi_ref), (wfd_hbm, wfd_ref)],
                      stage, stage_sem)

    @pl.when(first_tile)
    def _():
        kd_scr[:, 0:WINDOW, :] = jnp.zeros((N_KV_HEADS, WINDOW, LANES), kd_scr.dtype)
        vt_scr[:, 0:WINDOW] = jnp.zeros((KV_WIDTH, WINDOW), vt_scr.dtype)
        z_scr[0:CONV_HALO, :] = jnp.zeros((CONV_HALO, CONV_CHANNELS), z_scr.dtype)

    def body(mixer, swiglu):
        acts, ffn = {}, []
        if swiglu:
            x1_prev = x1_scr[...]
            h2 = _rmsnorm(x1_prev, gffn_ref[...]).astype(jnp.bfloat16)
            ffn.append(x1_prev)

        def ffn_up(j):
            if swiglu:
                c0 = j * FFN_CHUNK
                gate = _dot(h2, wfi_ref[:, c0:c0 + FFN_CHUNK])
                up = _dot(h2, wfi_ref[:, D_FF + c0:D_FF + c0 + FFN_CHUNK])
                acts[j] = (_silu(gate) * up).astype(jnp.bfloat16)

        def ffn_down(first, last):
            if swiglu:
                a = jnp.concatenate([acts.pop(j) for j in range(first, last + 1)], axis=1)
                ffn[0] = ffn[0] + _dot(a, wfd_ref[first * FFN_CHUNK:(last + 1) * FFN_CHUNK, :])

        assert N_FFN_CHUNKS == 11, "the matmul sequence below places chunks 0..10 by hand"
        if mixer:
            x = x_ref[0]
            h = _rmsnorm(x, gmix_ref[...]).astype(jnp.bfloat16)
            glu = _dot(h, win_ref[:, GLU_OFF:GATE_OFF]) + bin_ref[:, GLU_OFF:GATE_OFF]
            qkv = _dot(h, win_ref[:, 0:GLU_OFF]) + bin_ref[:, 0:GLU_OFF]
        ffn_up(0)
        if mixer:
            z_scr[CONV_HALO:CONV_HALO + T, :] = (glu[:, 0:CONV_CHANNELS]
                                                 * jax.nn.sigmoid(glu[:, CONV_CHANNELS:]))
            q = qkv[:, 0:K_OFF].astype(jnp.bfloat16)
            k = qkv[:, K_OFF:V_OFF] * (HEAD_DIM ** -0.5 * LOG2E)
            k_swapped = pltpu.roll(k, HEAD_DIM, 1)
            low_half = lax.broadcasted_iota(jnp.int32, (T, LANES), 1) < HEAD_DIM
            kd_scr[0, WINDOW:WINDOW + T, :] = jnp.where(low_half, k, k_swapped).astype(jnp.bfloat16)
            kd_scr[1, WINDOW:WINDOW + T, :] = jnp.where(low_half, k_swapped, k).astype(jnp.bfloat16)
            vt_scr[:, WINDOW:WINDOW + T] = qkv[:, V_OFF:GLU_OFF].T.astype(jnp.bfloat16)
            scores = _attn_scores(q, kd_scr)
            g_attn = jax.nn.sigmoid(_dot(h, win_ref[:, GATE_OFF:GATE_OFF + D_MODEL])
                                    + bin_ref[:, GATE_OFF:GATE_OFF + D_MODEL])
        ffn_up(1)
        if mixer:
            g_conv = jax.nn.sigmoid(_dot(h, win_ref[:, GATE_OFF + D_MODEL:IN_WIDTH])
                                    + bin_ref[:, GATE_OFF + D_MODEL:IN_WIDTH])
            probs = _attn_probs(scores, sinks_ref, first_tile)
            attn = _attn_values(probs, vt_scr, T)
            kd_scr[:, 0:WINDOW, :] = kd_scr[:, T:T + WINDOW, :]
            vt_scr[:, 0:WINDOW] = vt_scr[:, T:T + WINDOW]
        ffn_up(2)
        if mixer:
            y_attn = _dot(attn.astype(jnp.bfloat16), wap_ref[...])
        for j in range(3, N_FFN_CHUNKS):
            ffn_up(j)
        if mixer:
            acc = _causal_conv(z_scr, convw_ref, convb_ref, T)
            z_scr[0:CONV_HALO, :] = z_scr[T:T + CONV_HALO, :]
            mu = jnp.mean(acc, axis=-1, keepdims=True)
            xc = acc - mu
            var = jnp.mean(xc * xc, axis=-1, keepdims=True)
            c = xc * lax.rsqrt(var + EPS) * lng_ref[...] + lnb_ref[...]
            c = (c * jax.nn.sigmoid(c)).astype(jnp.bfloat16)
            y_conv = _dot(c, wcp_ref[...]) + bcp_ref[...]
        ffn_down(*FFN_DOWN_GROUPS[0])
        ffn_down(*FFN_DOWN_GROUPS[1])
        if mixer:
            merged = (g_attn * y_attn + g_conv * y_conv).astype(jnp.bfloat16)
        ffn_down(*FFN_DOWN_GROUPS[2])
        if mixer:
            x1_scr[...] = x + _dot(merged, wout_ref[...])
        if swiglu:
            o_ref[0] = _rmsnorm(ffn[0], gfin_ref[...])

    pl.when(step == 0)(lambda: body(True, False))
    pl.when((step > 0) & (step < n_tiles))(lambda: body(True, True))
    pl.when(step == n_tiles)(lambda: body(False, True))


def _resident(shape):
    return pl.BlockSpec(shape, lambda i: (0,) * len(shape), pipeline_mode=pl.Buffered(1))


@jax.jit
def kernel(x, g_mix_norm, w_in, b_in, sinks, conv_w, conv_b, ln_g, ln_b, w_attn_proj,
           w_conv_proj, b_conv_proj, w_out, g_ffn_norm, w_ffn_in, w_ffn_down, g_final):
    B, S, D = x.shape
    assert D == D_MODEL and S % SEQ_TILE == 0 and SEQ_TILE % WINDOW == 0
    assert w_in.shape[0] == 1, "single-layer kernel"
    tiles_per_seq = S // SEQ_TILE
    n_tiles = B * tiles_per_seq
    row = lambda a: a.reshape(1, -1)
    weights = (w_in[0], w_attn_proj[0], w_conv_proj[0], w_out[0], w_ffn_in[0], w_ffn_down[0])
    operands = (
        sinks[0],
        x,
        row(g_mix_norm[0]), weights[0], row(b_in[0]),
        conv_w[0], row(conv_b[0]), row(ln_g[0]), row(ln_b[0]),
        weights[1], weights[2], row(b_conv_proj[0]),
        weights[3], row(g_ffn_norm[0]),
        weights[4], weights[5], row(g_final),
    )

    def tile_block(t):
        return (t // tiles_per_seq, t % tiles_per_seq, 0)

    x_spec = pl.BlockSpec((1, SEQ_TILE, D), lambda i: tile_block(jnp.minimum(i, n_tiles - 1)))
    o_spec = pl.BlockSpec((1, SEQ_TILE, D), lambda i: tile_block(jnp.maximum(i - 1, 0)))
    in_specs = [pl.BlockSpec(memory_space=pltpu.SMEM), x_spec]
    in_specs += [pl.BlockSpec(memory_space=pl.ANY) if any(a is w for w in weights) else _resident(a.shape)
                 for a in operands[2:]]
    return pl.pallas_call(
        functools.partial(_layer_kernel, tiles_per_seq, n_tiles),
        out_shape=jax.ShapeDtypeStruct(x.shape, x.dtype),
        grid=(n_tiles + 1,),
        in_specs=in_specs,
        out_specs=o_spec,
        scratch_shapes=[
            pltpu.VMEM((N_KV_HEADS, WINDOW + SEQ_TILE, LANES), jnp.bfloat16),
            pltpu.VMEM((KV_WIDTH, WINDOW + SEQ_TILE), jnp.bfloat16),
            pltpu.VMEM((SEQ_TILE + CONV_HALO, CONV_CHANNELS), jnp.float32),
            pltpu.VMEM((SEQ_TILE, D_MODEL), jnp.float32),
            *[pltpu.VMEM(w.shape, jnp.bfloat16) for w in weights],
            pltpu.VMEM((STAGE_SLOTS, STAGE_ROWS, STAGE_COLS), jnp.float32),
            pltpu.SemaphoreType.DMA((STAGE_SLOTS,)),
        ],
        compiler_params=pltpu.CompilerParams(
            dimension_semantics=("arbitrary",),
            vmem_limit_bytes=VMEM_LIMIT_BYTES,
        ),
        name="hybrid_layer",
    )(*operands)
```
